```python
import math
import jax
import jax.numpy as jnp
from jax import lax
import numpy as np

D_MODEL = 1024
BATCH = 1
SEQ = 16384
DEPTH = 2
DEC_BATCH = 128
DEC_SEQ = 4
PAST_LEN = 16384
PAGE_SIZE = 128

NSA_HEADS = 8
NSA_DH = 64
CMP_BLOCK = 64
N_SEL = 16
WINDOW = 512
MLA_HEADS = 8
MLA_Q_RANK = 256
MLA_KV_RANK = 128
MLA_NOPE = 64
MLA_ROPE = 32
MLA_V = 64
ROPE_BASE = 10000.0
DIFF_HEADS = 4
DIFF_DH = 64
N_MEM = 256
MEM_HEADS = 4
MEM_DH = 64
N_BUCKETS = 32
MAX_DISTANCE = 2048
N_BIAS_HEADS = NSA_HEADS + DIFF_HEADS
D_FF = 3584
N_EXPERTS = 8
TOP_K = 2
BRANCH_W = NSA_HEADS * NSA_DH
Q_BLOCK = 128
EPS = 1e-6
NEG_INF = -1e30
IN_SPLITS = (NSA_HEADS * NSA_DH, 6 * NSA_DH, 3 * NSA_HEADS,
             MLA_Q_RANK, MLA_KV_RANK, MLA_ROPE,
             DIFF_HEADS * 2 * DIFF_DH, 2 * DIFF_DH, 2 * DIFF_DH,
             3 * D_MODEL)
D_IN = sum(IN_SPLITS)
IN_SPLIT_IDX = tuple(int(v) for v in np.cumsum(IN_SPLITS)[:-1])

kernel_name = 'hybrid_nsa_mla_diff_decoder_step'


def rmsnorm(x, g):
    xf = x.astype(jnp.float32)
    y = xf * lax.rsqrt(jnp.mean(xf * xf, axis=-1, keepdims=True) + EPS)
    return (y * g.astype(jnp.float32)).astype(x.dtype)


def masked_softmax(s, mask):
    s = jnp.where(mask, s.astype(jnp.float32), NEG_INF)
    p = jnp.exp(s - jnp.max(s, axis=-1, keepdims=True)) * mask
    return p / jnp.maximum(jnp.sum(p, axis=-1, keepdims=True), 1e-30)


def t5_bucket(dist):
    n = jnp.maximum(dist, 0)
    exact = N_BUCKETS // 2
    nf = jnp.maximum(n, 1).astype(jnp.float32)
    large = exact + (jnp.log(nf / exact) / math.log(MAX_DISTANCE / exact) * (N_BUCKETS - exact)).astype(jnp.int32)
    return jnp.where(n < exact, n, jnp.minimum(large, N_BUCKETS - 1))


def rel_bias(table, dist):
    return jnp.moveaxis(table[t5_bucket(dist)], -1, 0).astype(jnp.float32)


def rope(x, pos):
    half = x.shape[-1] // 2
    inv = ROPE_BASE ** (-jnp.arange(half, dtype=jnp.float32) / half)
    ang = pos.astype(jnp.float32)[..., None] * inv
    ang = ang.reshape(pos.shape + (1,) * (x.ndim - pos.ndim - 1) + (half,))
    cos, sin = jnp.cos(ang), jnp.sin(ang)
    x1 = x[..., :half].astype(jnp.float32)
    x2 = x[..., half:].astype(jnp.float32)
    return jnp.concatenate([x1 * cos - x2 * sin, x1 * sin + x2 * cos], axis=-1).astype(x.dtype)


def mixer_features(xn, pos, lw):
    n, t, _ = xn.shape
    h = jnp.einsum('ntd,de->nte', xn, lw['w_in'])
    nq, nkv, ng, cq, ckv, kr, dq, dk, dv, mg = jnp.split(h, IN_SPLIT_IDX, axis=-1)
    nq = rmsnorm(nq.reshape(n, t, NSA_HEADS, NSA_DH), lw['nsa_q_norm'])
    kc, vc, ks, vs, kw, vw = jnp.split(nkv, 6, axis=-1)
    ks = rmsnorm(ks, lw['nsa_k_norm'][1])
    kw = rmsnorm(kw, lw['nsa_k_norm'][2])
    ng = jax.nn.sigmoid(ng.reshape(n, t, NSA_HEADS, 3).astype(jnp.float32))
    q = jnp.einsum('ntr,re->nte', rmsnorm(cq, lw['mla_q_rank_norm']), lw['w_mla_uq'])
    q = q.reshape(n, t, MLA_HEADS, MLA_NOPE + MLA_ROPE)
    gq = lw['mla_norm_q']
    qn = rmsnorm(q[..., :MLA_NOPE], gq[:MLA_NOPE])
    qr = rope(rmsnorm(q[..., MLA_NOPE:], gq[MLA_NOPE:]), pos)
    ckv = rmsnorm(ckv, lw['mla_kv_rank_norm'])
    kr = rope(rmsnorm(kr, lw['mla_norm_k'][MLA_NOPE:]), pos)
    dq = rmsnorm(dq.reshape(n, t, DIFF_HEADS, 2, DIFF_DH), lw['diff_norm_q'])
    dk = rmsnorm(dk.reshape(n, t, 2, DIFF_DH), lw['diff_norm_k']).reshape(n, t, 2 * DIFF_DH)
    mg = jax.nn.sigmoid(mg.reshape(n, t, 3, D_MODEL).astype(jnp.float32))
    qf = (nq, ng, qn, qr, dq)
    rows = (jnp.concatenate([kc, vc], axis=-1), jnp.concatenate([ks, vs], axis=-1),
            jnp.concatenate([kw, vw], axis=-1), jnp.concatenate([ckv, kr], axis=-1),
            jnp.concatenate([dk, dv], axis=-1))
    return qf, rows, mg


def nsa_context(rows_cmp, rows_sel, g_cmp):
    tk = rows_cmp.shape[0]
    n_cmp = tk // CMP_BLOCK
    blk = rows_cmp[:n_cmp * CMP_BLOCK].reshape(n_cmp, CMP_BLOCK, 2 * NSA_DH)
    blk = jnp.mean(blk.astype(jnp.float32), axis=1).astype(rows_cmp.dtype)
    kc = rmsnorm(blk[:, :NSA_DH], g_cmp)
    vc = blk[:, NSA_DH:]
    n_pad = -(-tk // CMP_BLOCK)
    sel = jnp.pad(rows_sel, ((0, n_pad * CMP_BLOCK - tk), (0, 0))).reshape(n_pad, CMP_BLOCK, 2 * NSA_DH)
    return kc, vc, sel


def nsa_block(q, qpos, gates, kc, vc, sel, kw, vw, kwpos, table):
    scale = NSA_DH ** -0.5
    nq = q.shape[0]
    n_cmp = kc.shape[0]
    cur = qpos // CMP_BLOCK
    bidx = jnp.arange(n_cmp, dtype=jnp.int32)
    vis = bidx[None, :] < cur[:, None]
    dist_c = qpos[:, None] - (bidx[None, :] * CMP_BLOCK + CMP_BLOCK - 1)
    s = jnp.einsum('qhd,nd->hqn', q, kc) * scale + rel_bias(table, dist_c)
    p_c = masked_softmax(s, vis[None])
    o_cmp = jnp.einsum('hqn,nd->qhd', p_c, vc)
    score = jnp.where(vis, jnp.sum(p_c, axis=0), -1.0)
    top_s, top_i = lax.top_k(score, min(N_SEL - 1, n_cmp))
    idx = jnp.concatenate([top_i, cur[:, None]], axis=1)
    valid = jnp.concatenate([top_s >= 0.0, jnp.ones((nq, 1), dtype=bool)], axis=1)
    g = sel[idx]
    kpos = idx[..., None] * CMP_BLOCK + jnp.arange(CMP_BLOCK, dtype=jnp.int32)
    m = valid[..., None] & (kpos <= qpos[:, None, None])
    s = jnp.einsum('qhd,qnbd->hqnb', q, g[..., :NSA_DH]) * scale + rel_bias(table, qpos[:, None, None] - kpos)
    p = masked_softmax(s.reshape(NSA_HEADS, nq, -1), m.reshape(1, nq, -1))
    o_sel = jnp.einsum('hqk,qkd->qhd', p, g[..., NSA_DH:].reshape(nq, -1, NSA_DH))
    mw = (kwpos[None, :] <= qpos[:, None]) & (kwpos[None, :] > qpos[:, None] - WINDOW) & (kwpos[None, :] >= 0)
    s = jnp.einsum('qhd,kd->hqk', q, kw) * scale + rel_bias(table, qpos[:, None] - kwpos[None, :])
    o_win = jnp.einsum('hqk,kd->qhd', masked_softmax(s, mw[None]), vw)
    out = gates[..., 0:1] * o_cmp + gates[..., 1:2] * o_sel + gates[..., 2:3] * o_win
    return out.reshape(nq, NSA_HEADS * NSA_DH)


def mla_context(rows, lw):
    tk = rows.shape[0]
    ckv, k_rope = rows[:, :MLA_KV_RANK], rows[:, MLA_KV_RANK:]
    k_nope = jnp.einsum('kr,re->ke', ckv, lw['w_mla_uk']).reshape(tk, MLA_HEADS, MLA_NOPE)
    k_nope = rmsnorm(k_nope, lw['mla_norm_k'][:MLA_NOPE])
    v = jnp.einsum('kr,re->ke', ckv, lw['w_mla_uv']).reshape(tk, MLA_HEADS, MLA_V)
    return k_nope, k_rope, v


def mla_block(qn, qr, qpos, k_nope, k_rope, v, kpos):
    s = (jnp.einsum('qhd,khd->hqk', qn, k_nope) + jnp.einsum('qhd,kd->hqk', qr, k_rope)) * (MLA_NOPE + MLA_ROPE) ** -0.5
    p = masked_softmax(s, (kpos[None, :] <= qpos[:, None])[None])
    return jnp.einsum('hqk,khd->qhd', p, v).reshape(qpos.shape[0], MLA_HEADS * MLA_V)


def diff_block(q, qpos, k, v, kpos, table, lam, subln, lam_init):
    s = jnp.einsum('qhmd,kmd->mhqk', q, k) * DIFF_DH ** -0.5 + rel_bias(table, qpos[:, None] - kpos[None, :])
    p = masked_softmax(s, kpos[None, :] <= qpos[:, None])
    a = p[0] - lam * p[1]
    o = jnp.einsum('hqk,ke->qhe', a, v)
    o = rmsnorm(o, subln) * (1.0 - lam_init)
    return o.reshape(qpos.shape[0], DIFF_HEADS * 2 * DIFF_DH)


def attend_seq(nq, ng, qn, qr, dq, qpos, rows_cmp, rows_sel, rows_mla, rows_diff, kpos, win_fn,
               lw, tbl_nsa, tbl_diff, lam, lam_init):
    kc, vc, sel = nsa_context(rows_cmp, rows_sel, lw['nsa_k_norm'][0])
    k_nope, k_rope, v_mla = mla_context(rows_mla, lw)
    dk = rows_diff[:, :2 * DIFF_DH].reshape(-1, 2, DIFF_DH)
    dv = rows_diff[:, 2 * DIFF_DH:]
    tq = qpos.shape[0]
    qb = min(Q_BLOCK, tq)
    dt = nq.dtype

    def one_block(start):
        sl = lambda a: lax.dynamic_slice_in_dim(a, start, qb, axis=0)
        p = sl(qpos)
        kw, vw, kwpos = win_fn(start, qb)
        o_nsa = nsa_block(sl(nq), p, sl(ng), kc, vc, sel, kw, vw, kwpos, tbl_nsa)
        o_mla = mla_block(sl(qn), sl(qr), p, k_nope, k_rope, v_mla, kpos)
        o_diff = diff_block(sl(dq), p, dk, dv, kpos, tbl_diff, lam, lw['diff_subln'], lam_init)
        return jnp.stack([o_nsa.astype(dt), o_mla.astype(dt), o_diff.astype(dt)], axis=1)

    out = lax.map(one_block, jnp.arange(tq // qb, dtype=jnp.int32) * qb)
    return out.reshape(tq, 3, BRANCH_W)


def prompt_attention(qf, rows, lw, tbl_nsa, tbl_diff, lam, lam_init):
    sp = qf[0].shape[1]
    pos = jnp.arange(sp, dtype=jnp.int32)

    def per_seq(nq, ng, qn, qr, dq, rc, rs, rw, rm, rd):
        kw_pad = jnp.pad(rw, ((WINDOW, 0), (0, 0)))

        def win_fn(start, qb):
            blk = lax.dynamic_slice_in_dim(kw_pad, start, WINDOW + qb, axis=0)
            return blk[:, :NSA_DH], blk[:, NSA_DH:], start - WINDOW + jnp.arange(WINDOW + qb, dtype=jnp.int32)

        return attend_seq(nq, ng, qn, qr, dq, pos, rc, rs, rm, rd, pos, win_fn, lw, tbl_nsa, tbl_diff, lam, lam_init)

    return jax.vmap(per_seq)(*qf, *rows)


def sample_attention(qf, rows, cache_cmp, cache_sel, win_buf, cache_mla, cache_diff, page_table, l,
                     lw, tbl_nsa, tbl_diff, lam, lam_init):
    ds = qf[0].shape[1]
    qpos = PAST_LEN + jnp.arange(ds, dtype=jnp.int32)
    kpos = jnp.arange(PAST_LEN + ds, dtype=jnp.int32)
    wb = win_buf.shape[1]
    wpos = PAST_LEN - wb + jnp.arange(wb + ds, dtype=jnp.int32)

    def past(cache, pt):
        return cache[l, pt].reshape(pt.shape[0] * PAGE_SIZE, cache.shape[-1])

    def per_seq(args):
        nq, ng, qn, qr, dq, rc, rs, rw, rm, rd, buf, pt = args
        kw_all = jnp.concatenate([buf, rw], axis=0)
        win_fn = lambda start, qb: (kw_all[:, :NSA_DH], kw_all[:, NSA_DH:], wpos)
        return attend_seq(nq, ng, qn, qr, dq, qpos,
                          jnp.concatenate([past(cache_cmp, pt), rc], axis=0),
                          jnp.concatenate([past(cache_sel, pt), rs], axis=0),
                          jnp.concatenate([past(cache_mla, pt), rm], axis=0),
                          jnp.concatenate([past(cache_diff, pt), rd], axis=0),
                          kpos, win_fn, lw, tbl_nsa, tbl_diff, lam, lam_init)

    return lax.map(per_seq, (*qf, *rows, win_buf, page_table))


def merge_branches(br, mg, w_branch, w_out):
    h = jnp.einsum('ntbe,bed->ntbd', br, w_branch)
    return jnp.einsum('ntd,de->nte', jnp.sum(mg * h, axis=2).astype(br.dtype), w_out)


def mem_kv(mem, lw):
    n, m, _ = mem.shape
    kv = jnp.einsum('nmd,de->nme', rmsnorm(mem, lw['norm_mem_m']), lw['w_mem_kv'])
    k = rmsnorm(kv[..., :MEM_HEADS * MEM_DH].reshape(n, m, MEM_HEADS, MEM_DH), lw['mem_norm_qk'][1])
    return jnp.concatenate([k.reshape(n, m, MEM_HEADS * MEM_DH), kv[..., MEM_HEADS * MEM_DH:]], axis=-1)


def cross_attend(xn, kv, lw):
    n, t, _ = xn.shape
    q = jnp.einsum('ntd,de->nte', xn, lw['w_mem_q']).reshape(n, t, MEM_HEADS, MEM_DH)
    q = rmsnorm(q, lw['mem_norm_qk'][0])
    k = kv[..., :MEM_HEADS * MEM_DH].reshape(n, -1, MEM_HEADS, MEM_DH)
    v = kv[..., MEM_HEADS * MEM_DH:].reshape(n, -1, MEM_HEADS, MEM_DH)
    s = jnp.einsum('nthd,nmhd->nhtm', q, k).astype(jnp.float32) * MEM_DH ** -0.5
    p = jax.nn.softmax(s, axis=-1)
    o = jnp.einsum('nhtm,nmhd->nthd', p, v).reshape(n, t, MEM_HEADS * MEM_DH).astype(xn.dtype)
    return jnp.einsum('nte,ed->ntd', o, lw['w_mem_o'])


def swiglu(x, wg, wu, wd):
    h = jax.nn.silu(jnp.einsum('ntd,df->ntf', x, wg)) * jnp.einsum('ntd,df->ntf', x, wu)
    return jnp.einsum('ntf,fd->ntd', h, wd)


def moe_ffn(x, w_router, wg, wu, wd):
    logits = jnp.einsum('ntd,de->nte', x, w_router).astype(jnp.float32)
    top_v, top_i = lax.top_k(logits, TOP_K)
    w = jax.nn.softmax(top_v, axis=-1)
    gate = jnp.einsum('ntk,ntke->nte', w, jax.nn.one_hot(top_i, N_EXPERTS, dtype=jnp.float32))
    out = jnp.zeros(x.shape, jnp.float32)
    for e in range(N_EXPERTS):
        out = out + gate[..., e:e + 1] * swiglu(x, wg[e], wu[e], wd[e])
    return out.astype(x.dtype)


def setup_inputs(seed: int = 0) -> dict:
    key = jax.random.key(seed)
    keys = iter(jax.random.split(key, 48))
    f32 = jnp.float32

    def nrm(shape, scale=1.0):
        return jax.random.normal(next(keys), shape, f32) * scale

    def gain(shape):
        return 1.0 + nrm(shape, 0.02)

    n_pages = PAST_LEN // PAGE_SIZE
    n_pool = (DEC_BATCH * n_pages * 5) // 4
    wb = min(WINDOW, PAST_LEN)
    n_dense, n_moe = (DEPTH + 1) // 2, DEPTH // 2
    page_table = jax.random.permutation(next(keys), n_pool)[:DEC_BATCH * n_pages]
    page_table = page_table.reshape(DEC_BATCH, n_pages).astype(jnp.int32)
    return {
        'x_prompt': nrm((BATCH, SEQ, D_MODEL)),
        'x_sample': nrm((DEC_BATCH, DEC_SEQ, D_MODEL)),
        'cache_nsa_cmp': nrm((DEPTH, n_pool, PAGE_SIZE, 2 * NSA_DH)),
        'cache_nsa_sel': nrm((DEPTH, n_pool, PAGE_SIZE, 2 * NSA_DH)),
        'cache_nsa_win': nrm((DEPTH, DEC_BATCH, wb, 2 * NSA_DH)),
        'cache_mla': nrm((DEPTH, n_pool, PAGE_SIZE, MLA_KV_RANK + MLA_ROPE)),
        'cache_diff': nrm((DEPTH, n_pool, PAGE_SIZE, 4 * DIFF_DH)),
        'cache_mem': nrm((DEPTH, DEC_BATCH, N_MEM, 2 * MEM_HEADS * MEM_DH)),
        'page_table': page_table,
        'mem_prompt': nrm((BATCH, N_MEM, D_MODEL)),
        'rel_bias_table': nrm((N_BUCKETS, N_BIAS_HEADS), 0.5),
        'norm_mix': gain((DEPTH, D_MODEL)),
        'w_in': nrm((DEPTH, D_MODEL, D_IN), D_MODEL ** -0.5),
        'nsa_q_norm': gain((DEPTH, NSA_DH)),
        'nsa_k_norm': gain((DEPTH, 3, NSA_DH)),
        'mla_q_rank_norm': gain((DEPTH, MLA_Q_RANK)),
        'mla_kv_rank_norm': gain((DEPTH, MLA_KV_RANK)),
        'w_mla_uq': nrm((DEPTH, MLA_Q_RANK, MLA_HEADS * (MLA_NOPE + MLA_ROPE)), MLA_Q_RANK ** -0.5),
        'w_mla_uk': nrm((DEPTH, MLA_KV_RANK, MLA_HEADS * MLA_NOPE), MLA_KV_RANK ** -0.5),
        'w_mla_uv': nrm((DEPTH, MLA_KV_RANK, MLA_HEADS * MLA_V), MLA_KV_RANK ** -0.5),
        'mla_norm_q': gain((DEPTH, MLA_NOPE + MLA_ROPE)),
        'mla_norm_k': gain((DEPTH, MLA_NOPE + MLA_ROPE)),
        'diff_norm_q': gain((DEPTH, 2, DIFF_DH)),
        'diff_norm_k': gain((DEPTH, 2, DIFF_DH)),
        'diff_lambda': nrm((DEPTH, 4, DIFF_DH), 0.1),
        'diff_subln': gain((DEPTH, 2 * DIFF_DH)),
        'w_branch': nrm((DEPTH, 3, BRANCH_W, D_MODEL), BRANCH_W ** -0.5),
        'w_out': nrm((DEPTH, D_MODEL, D_MODEL), D_MODEL ** -0.5),
        'norm_mem_x': gain((DEPTH, D_MODEL)),
        'norm_mem_m': gain((DEPTH, D_MODEL)),
        'w_mem_q': nrm((DEPTH, D_MODEL, MEM_HEADS * MEM_DH), D_MODEL ** -0.5),
        'w_mem_kv': nrm((DEPTH, D_MODEL, 2 * MEM_HEADS * MEM_DH), D_MODEL ** -0.5),
        'mem_norm_qk': gain((DEPTH, 2, MEM_DH)),
        'w_mem_o': nrm((DEPTH, MEM_HEADS * MEM_DH, D_MODEL), (MEM_HEADS * MEM_DH) ** -0.5),
        'norm_ffn': gain((DEPTH, D_MODEL)),
        'w_ffn_gate': nrm((n_dense, D_MODEL, D_FF), D_MODEL ** -0.5),
        'w_ffn_up': nrm((n_dense, D_MODEL, D_FF), D_MODEL ** -0.5),
        'w_ffn_down': nrm((n_dense, D_FF, D_MODEL), D_FF ** -0.5),
        'w_router': nrm((n_moe, D_MODEL, N_EXPERTS), D_MODEL ** -0.5),
        'w_exp_gate': nrm((n_moe, N_EXPERTS, D_MODEL, D_FF), D_MODEL ** -0.5),
        'w_exp_up': nrm((n_moe, N_EXPERTS, D_MODEL, D_FF), D_MODEL ** -0.5),
        'w_exp_down': nrm((n_moe, N_EXPERTS, D_FF, D_MODEL), D_FF ** -0.5),
    }


def reference(x_prompt, x_sample, cache_nsa_cmp, cache_nsa_sel, cache_nsa_win, cache_mla, cache_diff, cache_mem,
              page_table, mem_prompt, rel_bias_table, norm_mix, w_in, nsa_q_norm, nsa_k_norm, mla_q_rank_norm,
              mla_kv_rank_norm, w_mla_uq, w_mla_uk, w_mla_uv, mla_norm_q, mla_norm_k, diff_norm_q, diff_norm_k,
              diff_lambda, diff_subln, w_branch, w_out, norm_mem_x, norm_mem_m, w_mem_q, w_mem_kv, mem_norm_qk,
              w_mem_o, norm_ffn, w_ffn_gate, w_ffn_up, w_ffn_down, w_router, w_exp_gate, w_exp_up, w_exp_down):
    bp, sp = x_prompt.shape[:2]
    db, ds = x_sample.shape[:2]
    pos_p = jnp.broadcast_to(jnp.arange(sp, dtype=jnp.int32), (bp, sp))
    pos_s = jnp.broadcast_to(PAST_LEN + jnp.arange(ds, dtype=jnp.int32), (db, ds))
    tbl_nsa = rel_bias_table[:, :NSA_HEADS]
    tbl_diff = rel_bias_table[:, NSA_HEADS:]
    xp, xs = x_prompt, x_sample
    p_cmp, p_sel, p_win, p_mla, p_diff, p_mem = [], [], [], [], [], []
    s_cmp, s_sel, s_win, s_mla, s_diff = [], [], [], [], []
    for l in range(DEPTH):
        lw = {'w_in': w_in[l], 'nsa_q_norm': nsa_q_norm[l], 'nsa_k_norm': nsa_k_norm[l],
              'mla_q_rank_norm': mla_q_rank_norm[l], 'mla_kv_rank_norm': mla_kv_rank_norm[l],
              'w_mla_uq': w_mla_uq[l], 'w_mla_uk': w_mla_uk[l], 'w_mla_uv': w_mla_uv[l],
              'mla_norm_q': mla_norm_q[l], 'mla_norm_k': mla_norm_k[l],
              'diff_norm_q': diff_norm_q[l], 'diff_norm_k': diff_norm_k[l], 'diff_subln': diff_subln[l],
              'norm_mem_m': norm_mem_m[l], 'w_mem_q': w_mem_q[l], 'w_mem_kv': w_mem_kv[l],
              'mem_norm_qk': mem_norm_qk[l], 'w_mem_o': w_mem_o[l]}
        lam_init = 0.8 - 0.6 * math.exp(-0.3 * l)
        lmb = diff_lambda[l].astype(jnp.float32)
        lam = jnp.exp(jnp.sum(lmb[0] * lmb[1])) - jnp.exp(jnp.sum(lmb[2] * lmb[3])) + lam_init
        qf_p, rows_p, mg_p = mixer_features(rmsnorm(xp, norm_mix[l]), pos_p, lw)
        qf_s, rows_s, mg_s = mixer_features(rmsnorm(xs, norm_mix[l]), pos_s, lw)
        br_p = prompt_attention(qf_p, rows_p, lw, tbl_nsa, tbl_diff, lam, lam_init)
        br_s = sample_attention(qf_s, rows_s, cache_nsa_cmp, cache_nsa_sel, cache_nsa_win[l], cache_mla,
                                cache_diff, page_table, l, lw, tbl_nsa, tbl_diff, lam, lam_init)
        xp = xp + merge_branches(br_p, mg_p, w_branch[l], w_out[l]).astype(xp.dtype)
        xs = xs + merge_branches(br_s, mg_s, w_branch[l], w_out[l]).astype(xs.dtype)
        kv_p = mem_kv(mem_prompt, lw)
        xp = xp + cross_attend(rmsnorm(xp, norm_mem_x[l]), kv_p, lw).astype(xp.dtype)
        xs = xs + cross_attend(rmsnorm(xs, norm_mem_x[l]), cache_mem[l], lw).astype(xs.dtype)
        hp, hs = rmsnorm(xp, norm_ffn[l]), rmsnorm(xs, norm_ffn[l])
        j = l // 2
        if l % 2 == 0:
            xp = xp + swiglu(hp, w_ffn_gate[j], w_ffn_up[j], w_ffn_down[j]).astype(xp.dtype)
            xs = xs + swiglu(hs, w_ffn_gate[j], w_ffn_up[j], w_ffn_down[j]).astype(xs.dtype)
        else:
            xp = xp + moe_ffn(hp, w_router[j], w_exp_gate[j], w_exp_up[j], w_exp_down[j])
            xs = xs + moe_ffn(hs, w_router[j], w_exp_gate[j], w_exp_up[j], w_exp_down[j])
        p_cmp.append(rows_p[0])
        p_sel.append(rows_p[1])
        p_win.append(rows_p[2][:, -min(WINDOW, sp):])
        p_mla.append(rows_p[3])
        p_diff.append(rows_p[4])
        p_mem.append(kv_p)
        s_cmp.append(rows_s[0])
        s_sel.append(rows_s[1])
        s_win.append(rows_s[2])
        s_mla.append(rows_s[3])
        s_diff.append(rows_s[4])
    return (xp, xs,
            jnp.stack(p_cmp), jnp.stack(p_sel), jnp.stack(p_win), jnp.stack(p_mla), jnp.stack(p_diff), jnp.stack(p_mem),
            jnp.stack(s_cmp), jnp.stack(s_sel), jnp.stack(s_win), jnp.stack(s_mla), jnp.stack(s_diff))
```

```python
import functools
import math

import numpy as np
import jax
import jax.numpy as jnp
from jax import lax
from jax.experimental import pallas as pl
from jax.experimental.pallas import tpu as pltpu

F32 = jnp.float32
BF16 = jnp.bfloat16
I32 = jnp.int32

NSA_HEADS = 8
NSA_DH = 64
CMP_BLOCK = 64
N_SEL = 16
WINDOW = 512
MLA_HEADS = 8
MLA_NOPE = 64
MLA_ROPE = 32
MLA_V = 64
MLA_KV_RANK = 128
MLA_Q_RANK = 256
ROPE_BASE = 10000.0
DIFF_HEADS = 4
DIFF_DH = 64
MEM_HEADS = 4
MEM_DH = 64
N_BUCKETS = 32
MAX_DISTANCE = 2048
N_EXPERTS = 8
EPS = 1e-6
NEG = -1e30

LANES = 128
SUBLANES = 8
VMEM_LIMIT = 56 * 1024 * 1024

ATT_TILE = 256
MLA_TILE = 512
ATT_KBLOCK = 1024
DEC_PAGES = 16
DEC_Q = 8
NEW_PAD = 128
SEL_COLS_MIN = 128

NT_DIMS = (((1,), (1,)), ((), ()))


def _cparams(sem):
    return pltpu.CompilerParams(dimension_semantics=sem, vmem_limit_bytes=VMEM_LIMIT)


def _round_up(x, m):
    return (x + m - 1) // m * m


def _bucket_thresholds():
    n = np.arange(0, 2 * MAX_DISTANCE, dtype=np.int64)
    exact = N_BUCKETS // 2
    nf = np.maximum(n, 1).astype(np.float32)
    large = exact + (np.log(nf / np.float32(exact)) / np.float32(math.log(MAX_DISTANCE / exact))
                     * np.float32(N_BUCKETS - exact)).astype(np.int32)
    b = np.where(n < exact, n, np.minimum(large, N_BUCKETS - 1))
    assert np.all(np.diff(b) >= 0)
    return tuple(int(np.argmax(b >= j)) for j in range(1, N_BUCKETS))


BUCKET_THR = _bucket_thresholds()
BIAS_FLAT_DIST = BUCKET_THR[-1]


def _bias_from_dist(d, tbl_ref, col):
    v = jnp.full(d.shape, tbl_ref[0, col], F32)
    for j in range(1, N_BUCKETS):
        v = jnp.where(d >= BUCKET_THR[j - 1], tbl_ref[j, col], v)
    return v


def _split_hi_lo(a):
    hi = a.astype(BF16)
    lo = (a - hi.astype(F32)).astype(BF16)
    return hi, lo


def _dot_hl(a, b01):
    hi, lo = _split_hi_lo(a)
    return (jnp.dot(hi, b01, preferred_element_type=F32)
            + jnp.dot(lo, b01, preferred_element_type=F32))


def _rms(x, g):
    return x * lax.rsqrt(jnp.mean(x * x, axis=-1, keepdims=True) + EPS) * g


def _group_rms(y, ind, ind_t, gain, gsize):
    ssq = _dot_hl(y * y, ind)
    rs = lax.rsqrt(ssq * (1.0 / gsize) + EPS)
    return y * _dot_hl(rs, ind_t) * gain


def _group_indicator(n, gsize):
    ind = np.zeros((n, LANES), np.float32)
    ind[np.arange(n), np.arange(n) // gsize] = 1.0
    return jnp.asarray(ind, BF16), jnp.asarray(ind.T.copy(), BF16)


def _online_softmax_update(s, v, m_ref, l_ref, acc_ref):
    m_prev = m_ref[...]
    m_new = jnp.maximum(m_prev, jnp.max(s, axis=-1, keepdims=True))
    alpha = jnp.exp(m_prev - m_new)
    p = jnp.exp(s - m_new)
    l_ref[...] = alpha * l_ref[...] + jnp.sum(p, axis=-1, keepdims=True)
    acc_ref[...] = alpha * acc_ref[...] + jnp.dot(p.astype(BF16), v, preferred_element_type=F32)
    m_ref[...] = m_new


def _strip_body(tbl_ref, o_ref, *, hoff, ar, ac, c0, dmax, cmax, rb):
    h = pl.program_id(0)
    cols = o_ref.shape[2]
    r = lax.broadcasted_iota(I32, (rb, cols), 0) + pl.program_id(1) * rb
    c = lax.broadcasted_iota(I32, (rb, cols), 1)
    d = ar * r + ac * c + c0
    v = _bias_from_dist(d, tbl_ref, hoff + h)
    ok = jnp.where(d >= 0, jnp.where(d <= dmax, jnp.where(c < cmax, 1, 0), 0), 0)
    o_ref[0] = jnp.where(ok > 0, v, NEG)


def _bias_strip(tbl, hoff, nh, rows, cols, ar, ac, c0, dmax=2 ** 30, cmax=2 ** 30):
    rb = rows if rows <= 256 else 256
    assert rows % rb == 0
    return pl.pallas_call(
        functools.partial(_strip_body, hoff=hoff, ar=ar, ac=ac, c0=c0, dmax=dmax, cmax=cmax, rb=rb),
        grid=(nh, rows // rb),
        in_specs=[pl.BlockSpec(memory_space=pltpu.SMEM)],
        out_specs=pl.BlockSpec((1, rb, cols), lambda h, r: (h, r, 0)),
        out_shape=jax.ShapeDtypeStruct((nh, rows, cols), F32),
        compiler_params=_cparams(("arbitrary", "arbitrary")),
        name="bias_strip",
    )(tbl)


def _mm_body(*refs, norm, epi, gsize):
    refs = list(refs)
    x_ref = refs.pop(0)
    g_ref = refs.pop(0) if norm else None
    w_ref = refs.pop(0)
    if epi == "grms":
        ind_ref, indt_ref, gain_ref = refs.pop(0), refs.pop(0), refs.pop(0)
    o_ref, xn_ref = refs

    @pl.when(pl.program_id(1) == 0)
    def _():
        x = x_ref[...]
        if norm:
            x = _rms(x, g_ref[...])
        xn_ref[...] = x.astype(BF16)

    y = jnp.dot(xn_ref[...], w_ref[...], preferred_element_type=F32)
    if epi == "sigmoid":
        y = jax.nn.sigmoid(y)
    elif epi == "grms":
        y = _group_rms(y, ind_ref[...], indt_ref[...], gain_ref[...], gsize)
    o_ref[...] = y.astype(o_ref.dtype)


def _mm(x, w, *, g=None, epi=None, gsize=1, gain=None, out_dtype=F32, tm=512, tn=None):
    m, k = x.shape
    n = w.shape[1]
    tm = min(tm, m)
    tn = n if (tn is None or epi == "grms") else tn
    assert m % tm == 0 and n % tn == 0
    args = [x]
    specs = [pl.BlockSpec((tm, k), lambda i, j: (i, 0))]
    if g is not None:
        args.append(g.reshape(1, k).astype(F32))
        specs.append(pl.BlockSpec((1, k), lambda i, j: (0, 0)))
    args.append(w.astype(BF16))
    specs.append(pl.BlockSpec((k, tn), lambda i, j: (0, j)))
    if epi == "grms":
        ind, ind_t = _group_indicator(n, gsize)
        args += [ind, ind_t, gain.reshape(1, n).astype(F32)]
        specs += [pl.BlockSpec((n, LANES), lambda i, j: (0, 0)),
                  pl.BlockSpec((LANES, n), lambda i, j: (0, 0)),
                  pl.BlockSpec((1, n), lambda i, j: (0, 0))]
    return pl.pallas_call(
        functools.partial(_mm_body, norm=g is not None, epi=epi, gsize=gsize),
        grid=(m // tm, n // tn),
        in_specs=specs,
        out_specs=pl.BlockSpec((tm, tn), lambda i, j: (i, j)),
        out_shape=jax.ShapeDtypeStruct((m, n), out_dtype),
        scratch_shapes=[pltpu.VMEM((tm, k), BF16)],
        compiler_params=_cparams(("arbitrary", "arbitrary")),
        name="norm_matmul",
    )(*args)


FA_NQ, FA_CMP, FA_SEL, FA_WIN, FA_CQ, FA_CKV, FA_DQ, FA_DK, FA_DV, FA_NG, FA_KR, FA_END = (
    0, 512, 640, 768, 896, 1152, 1280, 1792, 1920, 2048, 2176, 2304)
GN_NQ, GN_KS, GN_KW, GN_CQ, GN_CKV, GN_QN, GN_QR, GN_KR, GN_DQ, GN_DK, GN_END = (
    0, 512, 640, 768, 1024, 1152, 1664, 1920, 2048, 2560, 2688)


def _half_rms(x, g, lo_half):
    lane = lax.broadcasted_iota(I32, x.shape, 1)
    sel = (lane < NSA_DH) if lo_half else (lane >= NSA_DH)
    ssq = jnp.sum(jnp.where(sel, x * x, 0.0), axis=-1, keepdims=True)
    return jnp.where(sel, x * lax.rsqrt(ssq * (1.0 / NSA_DH) + EPS) * g, x)


def _feat_body(x_ref, gmix_ref, wa_ref, wuq_ref, cos_ref, sin_ref, gains_ref,
               i512_ref, i512t_ref, i16_ref, i16t_ref,
               cmp_ref, sel_ref, win_ref, mla_ref, diff_ref,
               nq_ref, gate_ref, qn_ref, qr_ref, dq_ref):
    gains = gains_ref[...]
    gslice = lambda a, b: gains[:, a:b]
    xn = _rms(x_ref[...], gmix_ref[...]).astype(BF16)
    h = jnp.dot(xn, wa_ref[...], preferred_element_type=F32)
    i512, i512t = i512_ref[...], i512t_ref[...]
    i16, i16t = i16_ref[...], i16t_ref[...]
    cos, sin = cos_ref[...], sin_ref[...]
    half = MLA_ROPE // 2

    nq_ref[...] = _group_rms(h[:, FA_NQ:FA_CMP], i512, i512t, gslice(GN_NQ, GN_KS), NSA_DH)
    cmp_ref[...] = h[:, FA_CMP:FA_SEL]
    sel_ref[...] = _half_rms(h[:, FA_SEL:FA_WIN], gslice(GN_KS, GN_KW), True)
    win_ref[...] = _half_rms(h[:, FA_WIN:FA_CQ], gslice(GN_KW, GN_CQ), True)
    gate_ref[...] = jax.nn.sigmoid(h[:, FA_NG:FA_KR])

    cqn = _rms(h[:, FA_CQ:FA_CKV], gslice(GN_CQ, GN_CKV)).astype(BF16)
    q = jnp.dot(cqn, wuq_ref[...], preferred_element_type=F32)
    nope_w = MLA_HEADS * MLA_NOPE
    qn_ref[...] = _group_rms(q[:, :nope_w], i512, i512t, gslice(GN_QN, GN_QR), MLA_NOPE)
    x1 = q[:, nope_w:nope_w + LANES]
    x2 = q[:, nope_w + LANES:nope_w + 2 * LANES]
    rs = lax.rsqrt(_dot_hl(x1 * x1 + x2 * x2, i16) * (1.0 / MLA_ROPE) + EPS)
    sc = _dot_hl(rs, i16t)
    x1 = x1 * sc * gslice(GN_QR, GN_QR + LANES)
    x2 = x2 * sc * gslice(GN_QR + LANES, GN_KR)
    qr_ref[:, 0:LANES] = x1 * cos - x2 * sin
    qr_ref[:, LANES:2 * LANES] = x1 * sin + x2 * cos
    mla_ref[:, 0:MLA_KV_RANK] = _rms(h[:, FA_CKV:FA_DQ], gslice(GN_CKV, GN_QN))
    kr = h[:, FA_KR:FA_END]
    krn = kr * lax.rsqrt(jnp.sum(kr * kr, axis=-1, keepdims=True) * (1.0 / MLA_ROPE) + EPS) * gslice(GN_KR, GN_DQ)
    k1, k2 = krn[:, 0:half], krn[:, half:MLA_ROPE]
    c16, s16 = cos[:, 0:half], sin[:, 0:half]
    mla_ref[:, MLA_KV_RANK:MLA_KV_RANK + MLA_ROPE] = jnp.concatenate(
        [k1 * c16 - k2 * s16, k1 * s16 + k2 * c16], axis=-1)

    dq_ref[...] = _group_rms(h[:, FA_DQ:FA_DK], i512, i512t, gslice(GN_DQ, GN_DK), DIFF_DH)
    dk = h[:, FA_DK:FA_DV]
    gdk = gslice(GN_DK, GN_END)
    dkn = _half_rms(_half_rms(dk, gdk, True), gdk, False)
    diff_ref[:, 0:2 * DIFF_DH] = dkn
    diff_ref[:, 2 * DIFF_DH:4 * DIFF_DH] = h[:, FA_DV:FA_NG]


def _features(x, pos, lw, tm=256):
    t, d = x.shape
    tm = min(tm, t)
    assert t % tm == 0
    half = MLA_ROPE // 2
    inv = ROPE_BASE ** (-jnp.arange(half, dtype=F32) / half)
    ang = pos.astype(F32)[:, None] * inv
    cos = jnp.tile(jnp.cos(ang), (1, LANES // half))
    sin = jnp.tile(jnp.sin(ang), (1, LANES // half))
    i512, i512t = _group_indicator(512, 64)
    i16, i16t = _group_indicator(LANES, half)
    row = lambda w: pl.BlockSpec((tm, w), lambda i: (i, 0))
    full = lambda a: pl.BlockSpec(a.shape, lambda i: (0,) * a.ndim)
    consts = [lw["g_mix"], lw["w_a"], lw["w_uq"]]
    tail = [lw["gains"], i512, i512t, i16, i16t]
    outs = [(128, "cmp"), (128, "sel"), (128, "win"), (MLA_KV_RANK + MLA_ROPE, "mla"), (256, "diff"),
            (512, "nq"), (128, "gate"), (512, "qn"), (256, "qr"), (512, "dq")]
    res = pl.pallas_call(
        _feat_body,
        grid=(t // tm,),
        in_specs=[row(d)] + [full(a) for a in consts] + [row(LANES), row(LANES)] + [full(a) for a in tail],
        out_specs=[row(w) for w, _ in outs],
        out_shape=[jax.ShapeDtypeStruct((t, w), F32) for w, _ in outs],
        compiler_params=_cparams(("arbitrary",)),
        name="mixer_features",
    )(x, *consts, cos, sin, *tail)
    return {name: r for (_, name), r in zip(outs, res)}


def _blockmean_body(pt_ref, *refs, npg, page):
    del pt_ref
    pages, g_ref, o_ref = refs[:npg], refs[npg], refs[npg + 1]
    per = page // CMP_BLOCK
    for p in range(npg):
        pg = pages[p][0, 0]
        o_ref[0, p * per:(p + 1) * per, :] = jnp.mean(pg.reshape(per, CMP_BLOCK, pg.shape[-1]), axis=1)
    o_ref[0] = _half_rms(o_ref[0], g_ref[...], True)


def _block_means(pool, layer, page_table, g_cmp):
    nb, n_pages = page_table.shape
    page, w = pool.shape[2], pool.shape[3]
    npg = min(DEC_PAGES, n_pages)
    assert n_pages % npg == 0
    per = page // CMP_BLOCK

    def pspec(p):
        return pl.BlockSpec((1, 1, page, w), lambda b, c, pt: (layer, pt[b, c * npg + p], 0, 0))

    gain = jnp.concatenate([g_cmp.astype(F32), jnp.ones((NSA_DH,), F32)]).reshape(1, 2 * NSA_DH)
    return pl.pallas_call(
        functools.partial(_blockmean_body, npg=npg, page=page),
        grid_spec=pltpu.PrefetchScalarGridSpec(
            num_scalar_prefetch=1,
            grid=(nb, n_pages // npg),
            in_specs=[pspec(p) for p in range(npg)] + [pl.BlockSpec((1, w), lambda b, c, pt: (0, 0))],
            out_specs=pl.BlockSpec((1, npg * per, w), lambda b, c, pt: (b, c, 0)),
        ),
        out_shape=jax.ShapeDtypeStruct((nb, n_pages * per, w), F32),
        compiler_params=_cparams(("arbitrary", "arbitrary")),
        name="block_means",
    )(page_table, *([pool] * npg), gain)


def _cmp_body(tbl_ref, q_ref, kc_ref, vc_ref, o_ref, sel_ref, *, nh, tq, qbase, n_cmp, add_cur):
    ncol = kc_ref.shape[1]
    qpos = qbase + pl.program_id(1) * tq + lax.broadcasted_iota(I32, (tq, ncol), 0)
    col = lax.broadcasted_iota(I32, (tq, ncol), 1)
    cur = qpos // CMP_BLOCK
    vis = jnp.where(col < cur, jnp.where(col < n_cmp, 1.0, 0.0), 0.0)
    dist = qpos - (col * CMP_BLOCK + CMP_BLOCK - 1)
    psum = jnp.zeros((tq, ncol), F32)
    for h in range(nh):
        s = lax.dot_general(q_ref[0, h], kc_ref[0], NT_DIMS, preferred_element_type=F32)
        s = jnp.where(vis > 0, s + _bias_from_dist(dist, tbl_ref, h), NEG)
        p = jnp.exp(s - jnp.max(s, axis=-1, keepdims=True)) * vis
        p = p / jnp.maximum(jnp.sum(p, axis=-1, keepdims=True), 1e-30)
        o_ref[0, h] = jnp.dot(p.astype(BF16), vc_ref[0], preferred_element_type=F32)
        psum = psum + p

    score = jnp.where(vis > 0, psum, -1.0)
    chosen = jnp.zeros((tq, ncol), F32)
    for _ in range(min(N_SEL - 1, n_cmp)):
        m = jnp.max(score, axis=-1, keepdims=True)
        first = jnp.min(jnp.where(score == m, col, ncol), axis=-1, keepdims=True)
        hit = col == first
        chosen = jnp.where(hit, jnp.where(m >= 0.0, 1.0, 0.0), chosen)
        score = jnp.where(hit, -2.0, score)
    if add_cur:
        chosen = jnp.where(col == cur, 1.0, chosen)
    sel_ref[0] = chosen.astype(BF16)


def _cmp_attention(tbl, q, kc, vc, *, tq, qbase, n_cmp, add_cur):
    nb, nh, t, _ = q.shape
    ncol = kc.shape[1]
    return pl.pallas_call(
        functools.partial(_cmp_body, nh=nh, tq=tq, qbase=qbase, n_cmp=n_cmp, add_cur=add_cur),
        grid=(nb, t // tq),
        in_specs=[pl.BlockSpec(memory_space=pltpu.SMEM),
                  pl.BlockSpec((1, nh, tq, NSA_DH), lambda b, i: (b, 0, i, 0)),
                  pl.BlockSpec((1, ncol, NSA_DH), lambda b, i: (b, 0, 0)),
                  pl.BlockSpec((1, ncol, NSA_DH), lambda b, i: (b, 0, 0))],
        out_specs=[pl.BlockSpec((1, nh, tq, NSA_DH), lambda b, i: (b, 0, i, 0)),
                   pl.BlockSpec((1, tq, ncol), lambda b, i: (b, i, 0))],
        out_shape=[jax.ShapeDtypeStruct((nb, nh, t, NSA_DH), F32),
                   jax.ShapeDtypeStruct((nb, t, ncol), BF16)],
        compiler_params=_cparams(("arbitrary", "arbitrary")),
        name="cmp_attention",
    )(tbl, q, kc, vc)


def _flash_body(*refs, ng, nh, gv, tq, nsub, scale, dclamp, wtiles, has_sel):
    refs = list(refs)
    q_ref, k_ref, v_ref, strip_ref = refs[:4]
    refs = refs[4:]
    if has_sel:
        sel_ref, e_ref = refs[:2]
        refs = refs[2:]
    o_ref, m_ref, l_ref, acc_ref = refs
    tk = tq
    i = pl.program_id(0)
    jj = pl.program_id(1)
    windowed = wtiles is not None
    if windowed:
        first = 0
        last = wtiles
    else:
        first = 0
        last = i // nsub

    @pl.when(jj == first)
    def _():
        m_ref[...] = jnp.full(m_ref.shape, NEG, F32)
        l_ref[...] = jnp.zeros(l_ref.shape, F32)
        acc_ref[...] = jnp.zeros(acc_ref.shape, F32)

    def tile_update(u, jt):
        r0 = pl.multiple_of(jnp.minimum((i - jt) * tk, dclamp), SUBLANES)
        bias = strip_ref[:, pl.ds(r0, tq), :]
        if has_sel:
            picked = jnp.dot(sel_ref[...], e_ref[:, u * tk:(u + 1) * tk], preferred_element_type=F32)
            bias = bias + ((picked - 1.0) * (-NEG))[None]
        for g in range(ng):
            q = q_ref[g].reshape(nh * tq, q_ref.shape[-1])
            k = k_ref[g, u * tk:(u + 1) * tk, :]
            s = lax.dot_general(q, k, NT_DIMS, preferred_element_type=F32)
            if scale != 1.0:
                s = s * scale
            s = (s.reshape(nh, tq, tk) + bias).reshape(nh * tq, tk)
            v = v_ref[g if gv > 1 else 0, u * tk:(u + 1) * tk, :]
            _online_softmax_update(s, v, m_ref.at[g], l_ref.at[g], acc_ref.at[g])

    for u in range(nsub):
        if windowed:
            jt = i - wtiles + jj
            cond = jt >= 0
        else:
            jt = jj * nsub + u
            cond = jt <= i
        pl.when(cond)(functools.partial(tile_update, u, jt))

    @pl.when(jj == last)
    def _():
        for g in range(ng):
            o = acc_ref[g] / l_ref[g]
            o_ref[g] = o.reshape(nh, tq, o.shape[-1])


def _flash(q, k, v, strip, *, tq, scale, dclamp, wtiles=None, sel=None, expand=None):
    ng, nh, t, d = q.shape
    gv, _, dv = v.shape
    tq = min(tq, t)
    assert t % tq == 0 and strip.shape[2] == tq and strip.shape[1] >= dclamp + tq
    if wtiles is None:
        nsub = max(1, min(ATT_KBLOCK, t) // tq)
        assert (t // tq) % nsub == 0
        grid = (t // tq, t // (tq * nsub))
        kblk = lambda i, j: jnp.minimum(j, i // nsub)
    else:
        nsub = 1
        grid = (t // tq, wtiles + 1)
        kblk = lambda i, j: jnp.maximum(i - wtiles + j, 0)
    tkb = tq * nsub
    args = [q, k, v, strip]
    specs = [pl.BlockSpec((ng, nh, tq, d), lambda i, j: (0, 0, i, 0)),
             pl.BlockSpec((ng, tkb, d), lambda i, j: (0, kblk(i, j), 0)),
             pl.BlockSpec((gv, tkb, dv), lambda i, j: (0, kblk(i, j), 0)),
             pl.BlockSpec(strip.shape, lambda i, j: (0, 0, 0))]
    if sel is not None:
        args += [sel, expand]
        specs += [pl.BlockSpec((tq, sel.shape[1]), lambda i, j: (i, 0)),
                  pl.BlockSpec((expand.shape[0], tkb), lambda i, j: (0, kblk(i, j)))]
    return pl.pallas_call(
        functools.partial(_flash_body, ng=ng, nh=nh, gv=gv, tq=tq, nsub=nsub, scale=scale,
                          dclamp=dclamp, wtiles=wtiles, has_sel=sel is not None),
        grid=grid,
        in_specs=specs,
        out_specs=pl.BlockSpec((ng, nh, tq, dv), lambda i, j: (0, 0, i, 0)),
        out_shape=jax.ShapeDtypeStruct((ng, nh, t, dv), F32),
        scratch_shapes=[pltpu.VMEM((ng, nh * tq, 1), F32), pltpu.VMEM((ng, nh * tq, 1), F32),
                        pltpu.VMEM((ng, nh * tq, dv), F32)],
        compiler_params=_cparams(("arbitrary", "arbitrary")),
        name="prompt_attention",
    )(*args)


def _decode_body(pt_ref, *refs, npg, page, kw, vlo, vhi, nh, has_sel):
    del pt_ref
    refs = list(refs)
    q_ref, bias_ref = refs[:2]
    refs = refs[2:]
    if has_sel:
        sel_ref, e_ref = refs[:2]
        refs = refs[2:]
    new_ref, nbias_ref = refs[:2]
    pages = refs[2:2 + npg]
    o_ref, cb_ref, m_ref, l_ref, acc_ref = refs[2 + npg:]
    c = pl.program_id(1)

    @pl.when(c == 0)
    def _():
        m_ref[...] = jnp.full(m_ref.shape, NEG, F32)
        l_ref[...] = jnp.zeros(l_ref.shape, F32)
        acc_ref[...] = jnp.zeros(acc_ref.shape, F32)

    for p in range(npg):
        cb_ref[p * page:(p + 1) * page, :] = pages[p][0, 0].astype(BF16)
    q = q_ref[0]
    s = lax.dot_general(q, cb_ref[:, 0:kw], NT_DIMS, preferred_element_type=F32) + bias_ref[...]
    if has_sel:
        picked = jnp.dot(sel_ref[0], e_ref[...], preferred_element_type=F32)
        nrow, ncol = s.shape
        s = (s.reshape(nh, nrow // nh, ncol) + ((picked - 1.0) * (-NEG))[None]).reshape(nrow, ncol)
    _online_softmax_update(s, cb_ref[:, vlo:vhi], m_ref, l_ref, acc_ref)

    @pl.when(c == pl.num_programs(1) - 1)
    def _():
        nw = new_ref[0].astype(BF16)
        s2 = lax.dot_general(q, nw[:, 0:kw], NT_DIMS, preferred_element_type=F32) + nbias_ref[...]
        _online_softmax_update(s2, nw[:, vlo:vhi], m_ref, l_ref, acc_ref)
        o_ref[0] = acc_ref[...] / l_ref[...]


def _decode_attention(pool, layer, page_table, q, bias, new_rows, new_bias, *, kw, vlo, vhi, nh,
                      sel=None, expand=None):
    nb, n_pages = page_table.shape
    page, w = pool.shape[2], pool.shape[3]
    r = q.shape[1]
    npg = min(DEC_PAGES, n_pages)
    assert n_pages % npg == 0
    ck = npg * page

    def pspec(p):
        return pl.BlockSpec((1, 1, page, w), lambda b, c, pt: (layer, pt[b, c * npg + p], 0, 0))

    args = [q, bias]
    specs = [pl.BlockSpec((1, r, kw), lambda b, c, pt: (b, 0, 0)),
             pl.BlockSpec((r, ck), lambda b, c, pt: (0, c))]
    if sel is not None:
        args += [sel, expand]
        specs += [pl.BlockSpec((1, DEC_Q, sel.shape[2]), lambda b, c, pt: (b, 0, 0)),
                  pl.BlockSpec((expand.shape[0], ck), lambda b, c, pt: (0, c))]
    args += [new_rows, new_bias]
    specs += [pl.BlockSpec((1, NEW_PAD, w), lambda b, c, pt: (b, 0, 0)),
              pl.BlockSpec((r, NEW_PAD), lambda b, c, pt: (0, 0))]
    args += [pool] * npg
    specs += [pspec(p) for p in range(npg)]
    dv = vhi - vlo
    return pl.pallas_call(
        functools.partial(_decode_body, npg=npg, page=page, kw=kw, vlo=vlo, vhi=vhi, nh=nh,
                          has_sel=sel is not None),
        grid_spec=pltpu.PrefetchScalarGridSpec(
            num_scalar_prefetch=1,
            grid=(nb, n_pages // npg),
            in_specs=specs,
            out_specs=pl.BlockSpec((1, r, dv), lambda b, c, pt: (b, 0, 0)),
            scratch_shapes=[pltpu.VMEM((ck, w), BF16), pltpu.VMEM((r, 1), F32), pltpu.VMEM((r, 1), F32),
                            pltpu.VMEM((r, dv), F32)],
        ),
        out_shape=jax.ShapeDtypeStruct((nb, r, dv), F32),
        compiler_params=_cparams(("arbitrary", "arbitrary")),
        name="decode_attention",
    )(page_table, *args)


def _mla_decode_body(pt_ref, qbd_ref, qr_ref, wuk_ref, wuv_ref, new_ref, nbias_ref, *refs, npg, page):
    del pt_ref
    pages = refs[:npg]
    o_ref, cb_ref, m_ref, l_ref, acc_ref = refs[npg:]
    c = pl.program_id(1)
    scale = (MLA_NOPE + MLA_ROPE) ** -0.5
    nrow = qbd_ref.shape[1]
    per = nrow // MLA_HEADS

    @pl.when(c == 0)
    def _():
        m_ref[...] = jnp.full(m_ref.shape, NEG, F32)
        l_ref[...] = jnp.zeros(l_ref.shape, F32)
        acc_ref[...] = jnp.zeros(acc_ref.shape, F32)

    def scores(lat, kr):
        n = lat.shape[0]
        knt = lax.dot_general(wuk_ref[...], lat, NT_DIMS, preferred_element_type=F32)
        ssq = jnp.sum((knt * knt).reshape(MLA_HEADS, MLA_NOPE, n), axis=1)
        rs = lax.rsqrt(ssq * (1.0 / MLA_NOPE) + EPS)
        sn = jnp.dot(qbd_ref[0], knt.astype(BF16), preferred_element_type=F32)
        sr = lax.dot_general(qr_ref[0], kr, NT_DIMS, preferred_element_type=F32)
        sn = (sn.reshape(MLA_HEADS, per, n) * rs[:, None, :]).reshape(nrow, n)
        return (sn + sr) * scale

    for p in range(npg):
        cb_ref[p * page:(p + 1) * page, :] = pages[p][0, 0].astype(BF16)
    lat = cb_ref[:, 0:MLA_KV_RANK]
    s = scores(lat, cb_ref[:, MLA_KV_RANK:MLA_KV_RANK + MLA_ROPE])
    _online_softmax_update(s, lat, m_ref, l_ref, acc_ref)

    @pl.when(c == pl.num_programs(1) - 1)
    def _():
        nw = new_ref[0].astype(BF16)
        lat2 = nw[:, 0:MLA_KV_RANK]
        s2 = scores(lat2, nw[:, MLA_KV_RANK:MLA_KV_RANK + MLA_ROPE]) + nbias_ref[...]
        _online_softmax_update(s2, lat2, m_ref, l_ref, acc_ref)
        ctx = (acc_ref[...] / l_ref[...]).astype(BF16)
        full = jnp.dot(ctx, wuv_ref[...], preferred_element_type=F32)
        for h in range(MLA_HEADS):
            o_ref[0, h * per:(h + 1) * per, :] = full[h * per:(h + 1) * per, h * MLA_V:(h + 1) * MLA_V]


def _mla_decode(pool, layer, page_table, qbd, qr, wuk_t, wuv, new_rows, new_bias):
    nb, n_pages = page_table.shape
    page, w = pool.shape[2], pool.shape[3]
    r = qbd.shape[1]
    npg = min(DEC_PAGES, n_pages)
    assert n_pages % npg == 0

    def pspec(p):
        return pl.BlockSpec((1, 1, page, w), lambda b, c, pt: (layer, pt[b, c * npg + p], 0, 0))

    full = lambda a: pl.BlockSpec(a.shape, lambda b, c, pt: (0,) * a.ndim)
    return pl.pallas_call(
        functools.partial(_mla_decode_body, npg=npg, page=page),
        grid_spec=pltpu.PrefetchScalarGridSpec(
            num_scalar_prefetch=1,
            grid=(nb, n_pages // npg),
            in_specs=[pl.BlockSpec((1, r, qbd.shape[2]), lambda b, c, pt: (b, 0, 0)),
                      pl.BlockSpec((1, r, MLA_ROPE), lambda b, c, pt: (b, 0, 0)),
                      full(wuk_t), full(wuv),
                      pl.BlockSpec((1, NEW_PAD, w), lambda b, c, pt: (b, 0, 0)),
                      full(new_bias)] + [pspec(p) for p in range(npg)],
            out_specs=pl.BlockSpec((1, r, MLA_V), lambda b, c, pt: (b, 0, 0)),
            scratch_shapes=[pltpu.VMEM((npg * page, w), BF16), pltpu.VMEM((r, 1), F32),
                            pltpu.VMEM((r, 1), F32), pltpu.VMEM((r, MLA_KV_RANK), F32)],
        ),
        out_shape=jax.ShapeDtypeStruct((nb, r, MLA_V), F32),
        compiler_params=_cparams(("arbitrary", "arbitrary")),
        name="mla_decode",
    )(page_table, qbd, qr, wuk_t, wuv, new_rows, new_bias, *([pool] * npg))


def _merge_body(lam_ref, x_ref, ocmp_ref, osel_ref, owin_ref, gate_ref, omla_ref, odiff_ref, subln_ref,
                mg_ref, wb_ref, wo_ref, o_ref, *, post_scale):
    d = x_ref.shape[1]
    gate = gate_ref[...]
    nsa = []
    for h in range(NSA_HEADS):
        nsa.append(gate[:, 3 * h:3 * h + 1] * ocmp_ref[h] + gate[:, 3 * h + 1:3 * h + 2] * osel_ref[h]
                   + gate[:, 3 * h + 2:3 * h + 3] * owin_ref[h])
    br_nsa = jnp.concatenate(nsa, axis=-1)
    br_mla = jnp.concatenate([omla_ref[h] for h in range(MLA_HEADS)], axis=-1)
    lam = lam_ref[0]
    dif = []
    for h in range(DIFF_HEADS):
        a = odiff_ref[0, h] - lam * odiff_ref[1, h]
        dif.append(_rms(a, subln_ref[...]) * post_scale)
    br_diff = jnp.concatenate(dif, axis=-1)
    acc = None
    for b, br in enumerate((br_nsa, br_mla, br_diff)):
        term = mg_ref[:, b * d:(b + 1) * d] * jnp.dot(br.astype(BF16), wb_ref[b], preferred_element_type=F32)
        acc = term if acc is None else acc + term
    o_ref[...] = x_ref[...] + jnp.dot(acc.astype(BF16), wo_ref[...], preferred_element_type=F32)


def _merge(x, o_cmp, o_sel, o_win, gate, o_mla, o_diff, lam, subln, mg, wb, wo, post_scale, tm=256):
    t, d = x.shape
    tm = min(tm, t)
    heads = lambda a: pl.BlockSpec((a.shape[0], tm, a.shape[2]), lambda i: (0, i, 0))
    row = lambda w: pl.BlockSpec((tm, w), lambda i: (i, 0))
    full = lambda a: pl.BlockSpec(a.shape, lambda i: (0,) * a.ndim)
    return pl.pallas_call(
        functools.partial(_merge_body, post_scale=post_scale),
        grid=(t // tm,),
        in_specs=[pl.BlockSpec(memory_space=pltpu.SMEM), row(d), heads(o_cmp), heads(o_sel), heads(o_win),
                  row(LANES), heads(o_mla),
                  pl.BlockSpec((2, DIFF_HEADS, tm, 2 * DIFF_DH), lambda i: (0, 0, i, 0)),
                  full(subln), row(3 * d), full(wb), full(wo)],
        out_specs=row(d),
        out_shape=jax.ShapeDtypeStruct((t, d), F32),
        compiler_params=_cparams(("arbitrary",)),
        name="merge_branches",
    )(lam, x, o_cmp, o_sel, o_win, gate, o_mla, o_diff, subln, mg, wb, wo)


def _xattn_body(x_ref, g_ref, wq_ref, gq_ref, k_ref, v_ref, wo_ref, o_ref):
    x = x_ref[0]
    xn = _rms(x, g_ref[...]).astype(BF16)
    acc = x
    for h in range(MEM_HEADS):
        q = _rms(jnp.dot(xn, wq_ref[h], preferred_element_type=F32), gq_ref[...]).astype(BF16)
        s = lax.dot_general(q, k_ref[0, h], NT_DIMS, preferred_element_type=F32) * MEM_DH ** -0.5
        p = jnp.exp(s - jnp.max(s, axis=-1, keepdims=True))
        p = p / jnp.sum(p, axis=-1, keepdims=True)
        o = jnp.dot(p.astype(BF16), v_ref[0, h], preferred_element_type=F32).astype(BF16)
        acc = acc + jnp.dot(o, wo_ref[h], preferred_element_type=F32)
    o_ref[0] = acc


def _cross_attend(x3, g, wq, gq, k4, v4, wo):
    nbk = k4.shape[0]
    nb, tm, d = x3.shape
    kidx = (lambda i: (i, 0, 0, 0)) if nbk > 1 else (lambda i: (0, 0, 0, 0))
    full = lambda a: pl.BlockSpec(a.shape, lambda i: (0,) * a.ndim)
    return pl.pallas_call(
        _xattn_body,
        grid=(nb,),
        in_specs=[pl.BlockSpec((1, tm, d), lambda i: (i, 0, 0)), full(g), full(wq), full(gq),
                  pl.BlockSpec((1,) + k4.shape[1:], kidx), pl.BlockSpec((1,) + v4.shape[1:], kidx), full(wo)],
        out_specs=pl.BlockSpec((1, tm, d), lambda i: (i, 0, 0)),
        out_shape=jax.ShapeDtypeStruct(x3.shape, F32),
        compiler_params=_cparams(("arbitrary",)),
        name="memory_cross_attention",
    )(x3, g, wq, gq, k4, v4, wo)


def _ffn_body(x_ref, g_ref, wg_ref, wu_ref, wd_ref, o_ref, xn_ref, acc_ref):
    f = pl.program_id(1)

    @pl.when(f == 0)
    def _():
        xn_ref[...] = _rms(x_ref[...], g_ref[...]).astype(BF16)
        acc_ref[...] = jnp.zeros(acc_ref.shape, F32)

    xn = xn_ref[...]
    hid = (jax.nn.silu(jnp.dot(xn, wg_ref[...], preferred_element_type=F32))
           * jnp.dot(xn, wu_ref[...], preferred_element_type=F32))
    acc_ref[...] += jnp.dot(hid.astype(BF16), wd_ref[...], preferred_element_type=F32)

    @pl.when(f == pl.num_programs(1) - 1)
    def _():
        o_ref[...] = x_ref[...] + acc_ref[...]


def _ffn_tiles(t, ff):
    tm = min(512, t)
    tf = 896 if ff % 896 == 0 else ff
    return tm, tf


def _dense_ffn(x, g, wg, wu, wd):
    t, d = x.shape
    ff = wg.shape[1]
    tm, tf = _ffn_tiles(t, ff)
    return pl.pallas_call(
        _ffn_body,
        grid=(t // tm, ff // tf),
        in_specs=[pl.BlockSpec((tm, d), lambda i, f: (i, 0)), pl.BlockSpec((1, d), lambda i, f: (0, 0)),
                  pl.BlockSpec((d, tf), lambda i, f: (0, f)), pl.BlockSpec((d, tf), lambda i, f: (0, f)),
                  pl.BlockSpec((tf, d), lambda i, f: (f, 0))],
        out_specs=pl.BlockSpec((tm, d), lambda i, f: (i, 0)),
        out_shape=jax.ShapeDtypeStruct((t, d), F32),
        scratch_shapes=[pltpu.VMEM((tm, d), BF16), pltpu.VMEM((tm, d), F32)],
        compiler_params=_cparams(("arbitrary", "arbitrary")),
        name="dense_ffn",
    )(x, g, wg, wu, wd)


def _moe_body(x_ref, g_ref, wrh_ref, wrl_ref, wg_ref, wu_ref, wd_ref, o_ref, xn_ref, gate_ref, acc_ref):
    e = pl.program_id(1)
    f = pl.program_id(2)

    @pl.when((e == 0) & (f == 0))
    def _():
        xf = _rms(x_ref[...], g_ref[...])
        xh, xl = _split_hi_lo(xf)
        xn_ref[...] = xh
        acc_ref[...] = jnp.zeros(acc_ref.shape, F32)
        logits = (jnp.dot(xh, wrh_ref[...], preferred_element_type=F32)
                  + jnp.dot(xl, wrh_ref[...], preferred_element_type=F32)
                  + jnp.dot(xh, wrl_ref[...], preferred_element_type=F32))
        lane = lax.broadcasted_iota(I32, logits.shape, 1)
        logits = jnp.where(lane < N_EXPERTS, logits, -jnp.inf)
        m1 = jnp.max(logits, axis=-1, keepdims=True)
        i1 = jnp.min(jnp.where(logits == m1, lane, LANES), axis=-1, keepdims=True)
        rest = jnp.where(lane == i1, -jnp.inf, logits)
        m2 = jnp.max(rest, axis=-1, keepdims=True)
        i2 = jnp.min(jnp.where(rest == m2, lane, LANES), axis=-1, keepdims=True)
        e2 = jnp.exp(m2 - m1)
        w1 = 1.0 / (1.0 + e2)
        w2 = e2 / (1.0 + e2)
        gate_ref[...] = jnp.where(lane == i1, w1, 0.0) + jnp.where(lane == i2, w2, 0.0)

    xn = xn_ref[...]
    lane = lax.broadcasted_iota(I32, gate_ref.shape, 1)
    ge = jnp.sum(jnp.where(lane == e, gate_ref[...], 0.0), axis=-1, keepdims=True)
    hid = (jax.nn.silu(jnp.dot(xn, wg_ref[0], preferred_element_type=F32))
           * jnp.dot(xn, wu_ref[0], preferred_element_type=F32))
    acc_ref[...] += jnp.dot((hid * ge).astype(BF16), wd_ref[0], preferred_element_type=F32)

    @pl.when((e == pl.num_programs(1) - 1) & (f == pl.num_programs(2) - 1))
    def _():
        o_ref[...] = x_ref[...] + acc_ref[...]


def _moe_ffn(x, g, w_router, wg, wu, wd):
    t, d = x.shape
    ne, _, ff = wg.shape
    tm, tf = _ffn_tiles(t, ff)
    wr = jnp.pad(w_router.astype(F32), ((0, 0), (0, LANES - ne)))
    wrh = wr.astype(BF16)
    wrl = (wr - wrh.astype(F32)).astype(BF16)
    return pl.pallas_call(
        _moe_body,
        grid=(t // tm, ne, ff // tf),
        in_specs=[pl.BlockSpec((tm, d), lambda i, e, f: (i, 0)), pl.BlockSpec((1, d), lambda i, e, f: (0, 0)),
                  pl.BlockSpec((d, LANES), lambda i, e, f: (0, 0)), pl.BlockSpec((d, LANES), lambda i, e, f: (0, 0)),
                  pl.BlockSpec((1, d, tf), lambda i, e, f: (e, 0, f)),
                  pl.BlockSpec((1, d, tf), lambda i, e, f: (e, 0, f)),
                  pl.BlockSpec((1, tf, d), lambda i, e, f: (e, f, 0))],
        out_specs=pl.BlockSpec((tm, d), lambda i, e, f: (i, 0)),
        out_shape=jax.ShapeDtypeStruct((t, d), F32),
        scratch_shapes=[pltpu.VMEM((tm, d), BF16), pltpu.VMEM((tm, LANES), F32), pltpu.VMEM((tm, d), F32)],
        compiler_params=_cparams(("arbitrary", "arbitrary", "arbitrary")),
        name="moe_ffn",
    )(x, g, wrh, wrl, wg, wu, wd)


IN_SPLITS = (512, 384, 24, 256, 128, 32, 512, 128, 128, 3072)


def _layer_weights(l, p):
    d = p["w_in"].shape[1]
    w_in = p["w_in"][l]
    offs = np.cumsum((0,) + IN_SPLITS)
    seg = lambda i: w_in[:, offs[i]:offs[i + 1]]
    padl = lambda a, n: jnp.pad(a, ((0, 0), (0, n - a.shape[1])))
    w_a = jnp.concatenate([seg(0), seg(1), seg(3), seg(4), seg(6), seg(7), seg(8),
                           padl(seg(2), LANES), padl(seg(5), LANES)], axis=1).astype(BF16)
    half = MLA_ROPE // 2
    per = MLA_NOPE + MLA_ROPE
    wuq = p["w_mla_uq"][l].reshape(MLA_Q_RANK, MLA_HEADS, per)
    w_uq = jnp.concatenate([wuq[:, :, :MLA_NOPE].reshape(MLA_Q_RANK, -1),
                            wuq[:, :, MLA_NOPE:MLA_NOPE + half].reshape(MLA_Q_RANK, -1),
                            wuq[:, :, MLA_NOPE + half:].reshape(MLA_Q_RANK, -1)], axis=1).astype(BF16)
    gq, gk = p["mla_norm_q"][l], p["mla_norm_k"][l]
    ones64 = jnp.ones((NSA_DH,), F32)
    gains = jnp.concatenate([
        jnp.tile(p["nsa_q_norm"][l], NSA_HEADS),
        p["nsa_k_norm"][l, 1], ones64, p["nsa_k_norm"][l, 2], ones64,
        p["mla_q_rank_norm"][l], p["mla_kv_rank_norm"][l],
        jnp.tile(gq[:MLA_NOPE], MLA_HEADS),
        jnp.tile(gq[MLA_NOPE:MLA_NOPE + half], MLA_HEADS), jnp.tile(gq[MLA_NOPE + half:], MLA_HEADS),
        gk[MLA_NOPE:], jnp.zeros((LANES - MLA_ROPE,), F32),
        jnp.tile(p["diff_norm_q"][l].reshape(-1), DIFF_HEADS),
        p["diff_norm_k"][l].reshape(-1)]).astype(F32).reshape(1, GN_END)
    return {
        "g_mix": p["norm_mix"][l].reshape(1, d).astype(F32),
        "w_a": w_a, "w_uq": w_uq, "gains": gains,
        "w_mg": seg(9).astype(BF16),
        "gk_nope": gk[:MLA_NOPE].astype(F32),
        "w_uk": p["w_mla_uk"][l].astype(BF16), "w_uv": p["w_mla_uv"][l].astype(BF16),
        "g_cmp": p["nsa_k_norm"][l, 0],
        "subln": p["diff_subln"][l].reshape(1, -1).astype(F32),
        "wb": p["w_branch"][l].astype(BF16), "wo": p["w_out"][l].astype(BF16),
    }


def _heads_major(a, nh):
    t = a.shape[0]
    return a.reshape(t, nh, a.shape[1] // nh).transpose(1, 0, 2)


def _pad_cols(a, n):
    return jnp.pad(a, [(0, 0)] * (a.ndim - 1) + [(0, n - a.shape[-1])])


def _mla_q_heads(qn, qr):
    half = MLA_ROPE // 2
    t = qn.shape[0]
    parts = [qn.reshape(t, MLA_HEADS, MLA_NOPE), qr[:, :LANES].reshape(t, MLA_HEADS, half),
             qr[:, LANES:].reshape(t, MLA_HEADS, half)]
    return _pad_cols(jnp.concatenate(parts, axis=-1), LANES).transpose(1, 0, 2)


def kernel(x_prompt, x_sample, cache_nsa_cmp, cache_nsa_sel, cache_nsa_win, cache_mla, cache_diff, cache_mem, page_table, mem_prompt, rel_bias_table, norm_mix, w_in, nsa_q_norm, nsa_k_norm, mla_q_rank_norm, mla_kv_rank_norm, w_mla_uq, w_mla_uk, w_mla_uv, mla_norm_q, mla_norm_k, diff_norm_q, diff_norm_k, diff_lambda, diff_subln, w_branch, w_out, norm_mem_x, norm_mem_m, w_mem_q, w_mem_kv, mem_norm_qk, w_mem_o, norm_ffn, w_ffn_gate, w_ffn_up, w_ffn_down, w_router, w_exp_gate, w_exp_up, w_exp_down):
    prm = dict(w_in=w_in, norm_mix=norm_mix, nsa_q_norm=nsa_q_norm, nsa_k_norm=nsa_k_norm,
               mla_q_rank_norm=mla_q_rank_norm, mla_kv_rank_norm=mla_kv_rank_norm, w_mla_uq=w_mla_uq,
               w_mla_uk=w_mla_uk, w_mla_uv=w_mla_uv, mla_norm_q=mla_norm_q, mla_norm_k=mla_norm_k,
               diff_norm_q=diff_norm_q, diff_norm_k=diff_norm_k, diff_subln=diff_subln,
               w_branch=w_branch, w_out=w_out)
    bp, sp, d = x_prompt.shape
    db, ds, _ = x_sample.shape
    depth = w_in.shape[0]
    n_pages = page_table.shape[1]
    page = cache_nsa_cmp.shape[2]
    past = n_pages * page
    wb_len = cache_nsa_win.shape[2]
    n_mem = mem_prompt.shape[1]
    assert bp == 1 and ds <= DEC_Q and past % CMP_BLOCK == 0 and wb_len % page == 0
    half = MLA_ROPE // 2

    tbl = rel_bias_table.astype(F32)
    tbl0 = jnp.zeros_like(tbl)
    tq = min(ATT_TILE, sp)
    tqm = min(MLA_TILE, sp)
    dclamp = _round_up(BIAS_FLAT_DIST + tq - 1, SUBLANES)
    strip_rows = _round_up(dclamp + tq, 256)
    s_sel = _bias_strip(tbl, 0, NSA_HEADS, strip_rows, tq, 1, -1, 0)
    s_diff = _bias_strip(tbl, NSA_HEADS, DIFF_HEADS, strip_rows, tq, 1, -1, 0)
    wtiles = -(-(WINDOW - 1) // tq)
    s_win = _bias_strip(tbl, 0, NSA_HEADS, wtiles * tq + tq, tq, 1, -1, 0, dmax=WINDOW - 1)
    s_mla = _bias_strip(tbl0, 0, 1, 2 * tqm, tqm, 1, -1, 0)
    rows_hq = lambda a: a.reshape(a.shape[0] * a.shape[1], a.shape[2])
    b_sel = rows_hq(_bias_strip(tbl, 0, NSA_HEADS, DEC_Q, past, 1, -1, past))
    b_diff1 = _bias_strip(tbl, NSA_HEADS, DIFF_HEADS, DEC_Q, past, 1, -1, past)
    b_diff = rows_hq(jnp.concatenate([b_diff1, b_diff1], axis=0))
    b_win = rows_hq(_bias_strip(tbl, 0, NSA_HEADS, DEC_Q, wb_len, 1, -1, wb_len, dmax=WINDOW - 1))
    nb_nsa = rows_hq(_bias_strip(tbl, 0, NSA_HEADS, DEC_Q, NEW_PAD, 1, -1, 0, cmax=ds))
    nb_diff1 = _bias_strip(tbl, NSA_HEADS, DIFF_HEADS, DEC_Q, NEW_PAD, 1, -1, 0, cmax=ds)
    nb_diff = rows_hq(jnp.concatenate([nb_diff1, nb_diff1], axis=0))
    nb_mla1 = _bias_strip(tbl0, 0, 1, DEC_Q, NEW_PAD, 1, -1, 0, cmax=ds)
    nb_mla = rows_hq(jnp.tile(nb_mla1, (MLA_HEADS, 1, 1)))

    nblk_p = sp // CMP_BLOCK
    ncol_p = max(SEL_COLS_MIN, _round_up(nblk_p, LANES))
    exp_p = (jnp.arange(ncol_p, dtype=I32)[:, None] == (jnp.arange(sp, dtype=I32) // CMP_BLOCK)[None]).astype(BF16)
    nblk_s = past // CMP_BLOCK
    ncol_s = max(SEL_COLS_MIN, _round_up(nblk_s, LANES))
    exp_s = (jnp.arange(ncol_s, dtype=I32)[:, None] == (jnp.arange(past, dtype=I32) // CMP_BLOCK)[None]).astype(BF16)

    xp = x_prompt.reshape(sp, d)
    xs = jnp.pad(x_sample, ((0, 0), (0, DEC_Q - ds), (0, 0))).reshape(db * DEC_Q, d)
    pos_p = jnp.arange(sp, dtype=I32)
    pos_s = jnp.tile(past + jnp.arange(DEC_Q, dtype=I32), db)
    pt = page_table.astype(I32)
    pt_prompt = jnp.arange(sp // page, dtype=I32)[None]
    pt_win = (jnp.arange(db, dtype=I32)[:, None] * (wb_len // page) + jnp.arange(wb_len // page, dtype=I32)[None])
    win_pool = cache_nsa_win.reshape(depth, db * (wb_len // page), page, cache_nsa_win.shape[3])

    outs = {k: [] for k in ("p_cmp", "p_sel", "p_win", "p_mla", "p_diff", "p_mem",
                            "s_cmp", "s_sel", "s_win", "s_mla", "s_diff")}
    for l in range(depth):
        lw = _layer_weights(l, prm)
        lam_init = 0.8 - 0.6 * math.exp(-0.3 * l)
        lmb = diff_lambda[l].astype(F32)
        lam = (jnp.exp(jnp.sum(lmb[0] * lmb[1])) - jnp.exp(jnp.sum(lmb[2] * lmb[3])) + lam_init).reshape(1)

        fp = _features(xp, pos_p, lw)
        fs = _features(xs, pos_s, lw)
        mg_p = _mm(xp, lw["w_mg"], g=lw["g_mix"], epi="sigmoid", tn=768)
        mg_s = _mm(xs, lw["w_mg"], g=lw["g_mix"], epi="sigmoid", tn=768)

        q_nsa = (_heads_major(fp["nq"], NSA_HEADS) * NSA_DH ** -0.5).astype(BF16)[None]
        bm = _block_means(fp["cmp"].reshape(1, sp // page, page, 2 * NSA_DH), 0, pt_prompt, lw["g_cmp"])
        bm = jnp.pad(bm, ((0, 0), (0, ncol_p - nblk_p), (0, 0))).astype(BF16)
        o_cmp, chosen = _cmp_attention(tbl, q_nsa, bm[:, :, :NSA_DH], bm[:, :, NSA_DH:], tq=tq, qbase=0,
                                       n_cmp=nblk_p, add_cur=True)
        sel16 = fp["sel"].astype(BF16)
        o_sel = _flash(q_nsa, sel16[None, :, :NSA_DH], sel16[None, :, NSA_DH:], s_sel, tq=tq, scale=1.0,
                       dclamp=dclamp, sel=chosen[0], expand=exp_p)
        win16 = fp["win"].astype(BF16)
        o_win = _flash(q_nsa, win16[None, :, :NSA_DH], win16[None, :, NSA_DH:], s_win, tq=tq, scale=1.0,
                       dclamp=wtiles * tq, wtiles=wtiles)
        lat_p = fp["mla"][:, :MLA_KV_RANK]
        kn = _mm(lat_p, lw["w_uk"], epi="grms", gsize=MLA_NOPE, gain=jnp.tile(lw["gk_nope"], MLA_HEADS),
                 out_dtype=BF16)
        vm = _mm(lat_p, lw["w_uv"], out_dtype=BF16)
        kr16 = fp["mla"][:, MLA_KV_RANK:].astype(BF16)
        k_mla = jnp.concatenate([_heads_major(kn, MLA_HEADS),
                                 jnp.broadcast_to(kr16[None], (MLA_HEADS, sp, MLA_ROPE))], axis=-1)
        k_mla = _pad_cols(k_mla, LANES)
        q_mla = _mla_q_heads(fp["qn"], fp["qr"]).astype(BF16)[:, None]
        o_mla = _flash(q_mla, k_mla, _heads_major(vm, MLA_HEADS), s_mla, tq=tqm,
                       scale=(MLA_NOPE + MLA_ROPE) ** -0.5, dclamp=tqm)
        dq = (fp["dq"].reshape(sp, DIFF_HEADS, 2, DIFF_DH).transpose(2, 1, 0, 3) * DIFF_DH ** -0.5).astype(BF16)
        dk = fp["diff"][:, :2 * DIFF_DH].reshape(sp, 2, DIFF_DH).transpose(1, 0, 2).astype(BF16)
        dv = fp["diff"][None, :, 2 * DIFF_DH:].astype(BF16)
        o_diff = _flash(dq, dk, dv, s_diff, tq=tq, scale=1.0, dclamp=dclamp)
        xp = _merge(xp, o_cmp[0], o_sel[0], o_win[0], fp["gate"], o_mla[:, 0], o_diff, lam, lw["subln"], mg_p,
                    lw["wb"], lw["wo"], 1.0 - lam_init)

        def new_rows(a):
            return jnp.pad(a.reshape(db, DEC_Q, a.shape[1])[:, :ds], ((0, 0), (0, NEW_PAD - ds), (0, 0)))

        nq_s = fs["nq"].reshape(db, DEC_Q, NSA_HEADS, NSA_DH).transpose(0, 2, 1, 3) * NSA_DH ** -0.5
        q_s4 = nq_s.astype(BF16)
        q_s = _pad_cols(nq_s.reshape(db, NSA_HEADS * DEC_Q, NSA_DH), LANES).astype(BF16)
        bm_s = _block_means(cache_nsa_cmp, l, pt, lw["g_cmp"])
        bm_s = jnp.pad(bm_s, ((0, 0), (0, ncol_s - nblk_s), (0, 0))).astype(BF16)
        o_cmp_s, chosen_s = _cmp_attention(tbl, q_s4, bm_s[:, :, :NSA_DH], bm_s[:, :, NSA_DH:], tq=DEC_Q,
                                           qbase=past, n_cmp=nblk_s, add_cur=False)
        o_sel_s = _decode_attention(cache_nsa_sel, l, pt, q_s, b_sel, new_rows(fs["sel"]), nb_nsa,
                                    kw=LANES, vlo=0, vhi=LANES, nh=NSA_HEADS, sel=chosen_s, expand=exp_s)
        o_win_s = _decode_attention(win_pool, l, pt_win, q_s, b_win, new_rows(fs["win"]), nb_nsa,
                                    kw=LANES, vlo=0, vhi=LANES, nh=NSA_HEADS)
        tok_major = lambda a: a.reshape(db, NSA_HEADS, DEC_Q, NSA_DH).transpose(1, 0, 2, 3).reshape(
            NSA_HEADS, db * DEC_Q, NSA_DH)
        o_sel_s = tok_major(o_sel_s[:, :, NSA_DH:])
        o_win_s = tok_major(o_win_s[:, :, NSA_DH:])
        o_cmp_s = o_cmp_s.transpose(1, 0, 2, 3).reshape(NSA_HEADS, db * DEC_Q, NSA_DH)
        qn_s = (fs["qn"] * jnp.tile(lw["gk_nope"], MLA_HEADS)).reshape(db, DEC_Q, MLA_HEADS, MLA_NOPE)
        qn_s = qn_s.transpose(0, 2, 1, 3)
        qbd = (qn_s[:, :, :, None, :] * jnp.eye(MLA_HEADS, dtype=F32)[None, :, None, :, None]).reshape(
            db, MLA_HEADS * DEC_Q, MLA_HEADS * MLA_NOPE).astype(BF16)
        qr_s = jnp.concatenate([fs["qr"][:, :LANES].reshape(db, DEC_Q, MLA_HEADS, half),
                                fs["qr"][:, LANES:].reshape(db, DEC_Q, MLA_HEADS, half)], axis=-1)
        qr_s = qr_s.transpose(0, 2, 1, 3).reshape(db, MLA_HEADS * DEC_Q, MLA_ROPE).astype(BF16)
        o_mla_s = _mla_decode(cache_mla, l, pt, qbd, qr_s, lw["w_uk"].T, lw["w_uv"], new_rows(fs["mla"]), nb_mla)
        o_mla_s = o_mla_s.reshape(db, MLA_HEADS, DEC_Q, MLA_V).transpose(1, 0, 2, 3).reshape(
            MLA_HEADS, db * DEC_Q, MLA_V)
        dq_s = fs["dq"].reshape(db, DEC_Q, DIFF_HEADS, 2, DIFF_DH).transpose(0, 3, 2, 1, 4) * DIFF_DH ** -0.5
        dq_s = dq_s.reshape(db, 2, DIFF_HEADS * DEC_Q, DIFF_DH)
        zero = jnp.zeros_like(dq_s[:, 0])
        q_d = jnp.concatenate([jnp.concatenate([dq_s[:, 0], zero], axis=-1),
                               jnp.concatenate([zero, dq_s[:, 1]], axis=-1)], axis=1).astype(BF16)
        o_diff_s = _decode_attention(cache_diff, l, pt, q_d, b_diff, new_rows(fs["diff"]), nb_diff,
                                     kw=2 * DIFF_DH, vlo=2 * DIFF_DH, vhi=4 * DIFF_DH, nh=2 * DIFF_HEADS)
        o_diff_s = o_diff_s.reshape(db, 2, DIFF_HEADS, DEC_Q, 2 * DIFF_DH).transpose(1, 2, 0, 3, 4).reshape(
            2, DIFF_HEADS, db * DEC_Q, 2 * DIFF_DH)
        xs = _merge(xs, o_cmp_s, o_sel_s, o_win_s, fs["gate"], o_mla_s, o_diff_s, lam, lw["subln"], mg_s,
                    lw["wb"], lw["wo"], 1.0 - lam_init)

        g_mem_k = jnp.tile(mem_norm_qk[l, 1], MEM_HEADS)
        hw = MEM_HEADS * MEM_DH
        k_mem = _mm(mem_prompt[0], w_mem_kv[l][:, :hw], g=norm_mem_m[l], epi="grms", gsize=MEM_DH, gain=g_mem_k)
        v_mem = _mm(mem_prompt[0], w_mem_kv[l][:, hw:], g=norm_mem_m[l])
        kv_p = jnp.concatenate([k_mem, v_mem], axis=-1)
        mem_heads = lambda a: a.reshape(a.shape[0], n_mem, MEM_HEADS, MEM_DH).transpose(0, 2, 1, 3).astype(BF16)
        wq4 = w_mem_q[l].reshape(d, MEM_HEADS, MEM_DH).transpose(1, 0, 2).astype(BF16)
        wo4 = w_mem_o[l].reshape(MEM_HEADS, MEM_DH, d).astype(BF16)
        g_x = norm_mem_x[l].reshape(1, d).astype(F32)
        g_q = mem_norm_qk[l, 0].reshape(1, MEM_DH).astype(F32)
        tmx = min(256, sp)
        xp = _cross_attend(xp.reshape(sp // tmx, tmx, d), g_x, wq4, g_q, mem_heads(k_mem[None]),
                           mem_heads(v_mem[None]), wo4).reshape(sp, d)
        cm = cache_mem[l]
        xs = _cross_attend(xs.reshape(db, DEC_Q, d), g_x, wq4, g_q, mem_heads(cm[:, :, :hw]),
                           mem_heads(cm[:, :, hw:]), wo4).reshape(db * DEC_Q, d)

        g_f = norm_ffn[l].reshape(1, d).astype(F32)
        j = l // 2
        if l % 2 == 0:
            wg, wu, wd = w_ffn_gate[j].astype(BF16), w_ffn_up[j].astype(BF16), w_ffn_down[j].astype(BF16)
            xp = _dense_ffn(xp, g_f, wg, wu, wd)
            xs = _dense_ffn(xs, g_f, wg, wu, wd)
        else:
            wg, wu, wd = w_exp_gate[j].astype(BF16), w_exp_up[j].astype(BF16), w_exp_down[j].astype(BF16)
            xp = _moe_ffn(xp, g_f, w_router[j], wg, wu, wd)
            xs = _moe_ffn(xs, g_f, w_router[j], wg, wu, wd)

        outs["p_cmp"].append(fp["cmp"][None])
        outs["p_sel"].append(fp["sel"][None])
        outs["p_win"].append(fp["win"][None, -min(WINDOW, sp):])
        outs["p_mla"].append(fp["mla"][None])
        outs["p_diff"].append(fp["diff"][None])
        outs["p_mem"].append(kv_p[None])
        srows = lambda a: a.reshape(db, DEC_Q, a.shape[1])[:, :ds]
        outs["s_cmp"].append(srows(fs["cmp"]))
        outs["s_sel"].append(srows(fs["sel"]))
        outs["s_win"].append(srows(fs["win"]))
        outs["s_mla"].append(srows(fs["mla"]))
        outs["s_diff"].append(srows(fs["diff"]))

    y_p = xp.reshape(bp, sp, d)
    y_s = xs.reshape(db, DEC_Q, d)[:, :ds]
    st = lambda k: jnp.stack(outs[k])
    return (y_p, y_s, st("p_cmp"), st("p_sel"), st("p_win"), st("p_mla"), st("p_diff"), st("p_mem"),
            st("s_cmp"), st("s_sel"), st("s_win"), st("s_mla"), st("s_diff"))
```

```python
import functools
import math

import numpy as np
import jax
import jax.numpy as jnp
from jax import lax
from jax.experimental import pallas as pl
from jax.experimental.pallas import tpu as pltpu

F32 = jnp.float32
BF16 = jnp.bfloat16
I32 = jnp.int32

NSA_HEADS = 8
NSA_DH = 64
CMP_BLOCK = 64
N_SEL = 16
WINDOW = 512
MLA_HEADS = 8
MLA_NOPE = 64
MLA_ROPE = 32
MLA_V = 64
MLA_KV_RANK = 128
MLA_Q_RANK = 256
ROPE_BASE = 10000.0
DIFF_HEADS = 4
DIFF_DH = 64
MEM_HEADS = 4
MEM_DH = 64
N_BUCKETS = 32
MAX_DISTANCE = 2048
N_EXPERTS = 8
EPS = 1e-6
NEG = -1e30
LOG2E = 1.4426950408889634

LANES = 128
SUBLANES = 8
VMEM_LIMIT = 56 * 1024 * 1024

ATT_TQ, ATT_TKB = 128, 1024
WIN_TQ, WIN_TKB = 256, 256
MLA_TQ, MLA_TKB = 512, 1024
FLASH_CHUNK = 256
CMP_TQ = 256
DEC_PAGES = 32
MLA_DEC_PAGES = 32
DEC_SPLIT = 2
DEC_Q = 8
NEW_PAD = 128
SEL_COLS_MIN = 128

NT_DIMS = (((1,), (1,)), ((), ()))


def _cparams(sem):
    return pltpu.CompilerParams(dimension_semantics=sem, vmem_limit_bytes=VMEM_LIMIT)


def _round_up(x, m):
    return (x + m - 1) // m * m


def _bucket_thresholds():
    n = np.arange(0, 2 * MAX_DISTANCE, dtype=np.int64)
    exact = N_BUCKETS // 2
    nf = np.maximum(n, 1).astype(np.float32)
    large = exact + (np.log(nf / np.float32(exact)) / np.float32(math.log(MAX_DISTANCE / exact))
                     * np.float32(N_BUCKETS - exact)).astype(np.int32)
    b = np.where(n < exact, n, np.minimum(large, N_BUCKETS - 1))
    assert np.all(np.diff(b) >= 0)
    return tuple(int(np.argmax(b >= j)) for j in range(1, N_BUCKETS))


BUCKET_THR = _bucket_thresholds()
BIAS_FLAT_DIST = BUCKET_THR[-1]


def _bias_from_dist(d, tbl_ref, col):
    v = jnp.full(d.shape, tbl_ref[0, col], F32)
    for j in range(1, N_BUCKETS):
        v = jnp.where(d >= BUCKET_THR[j - 1], tbl_ref[j, col], v)
    return v


def _split_hi_lo(a):
    hi = a.astype(BF16)
    lo = (a - hi.astype(F32)).astype(BF16)
    return hi, lo


def _dot_hl(a, b01):
    hi, lo = _split_hi_lo(a)
    return (jnp.dot(hi, b01, preferred_element_type=F32)
            + jnp.dot(lo, b01, preferred_element_type=F32))


def _rms(x, g):
    return x * lax.rsqrt(jnp.mean(x * x, axis=-1, keepdims=True) + EPS) * g


def _group_rms(y, ind, ind_t, gain, gsize):
    ssq = _dot_hl(y * y, ind)
    rs = lax.rsqrt(ssq * (1.0 / gsize) + EPS)
    return y * _dot_hl(rs, ind_t) * gain


def _group_indicator(n, gsize):
    ind = np.zeros((n, LANES), np.float32)
    ind[np.arange(n), np.arange(n) // gsize] = 1.0
    return jnp.asarray(ind, BF16), jnp.asarray(ind.T.copy(), BF16)


def _softmax_step(s, m_ref, l_ref):
    m_prev = m_ref[...]
    m_new = jnp.maximum(m_prev, jnp.max(s, axis=-1, keepdims=True))
    alpha = jnp.exp(m_prev - m_new)
    p = jnp.exp(s - pltpu.repeat(m_new, s.shape[1] // LANES, axis=1))
    l_ref[...] = alpha * l_ref[...] + jnp.sum(p, axis=-1, keepdims=True)
    m_ref[...] = m_new
    return p, alpha


def _init_softmax_state(m_ref, l_ref, acc_ref):
    m_ref[...] = jnp.full(m_ref.shape, NEG, F32)
    l_ref[...] = jnp.zeros(l_ref.shape, F32)
    acc_ref[...] = jnp.zeros(acc_ref.shape, F32)


def _merge_softmax_states(m_ref, l_ref, acc_ref):
    ns = m_ref.shape[0]
    dv = acc_ref.shape[-1]
    m_all = m_ref[0]
    for h in range(1, ns):
        m_all = jnp.maximum(m_all, m_ref[h])
    num = den = None
    for h in range(ns):
        w = jnp.exp(m_ref[h] - m_all)
        n_h, d_h = acc_ref[h] * w[:, :dv], l_ref[h] * w
        num, den = (n_h, d_h) if num is None else (num + n_h, den + d_h)
    return num / den[:, :dv]


def _strip_body(tbl_ref, o_ref, *, hoff, ar, ac, c0, dmax, cmax, rb, mult):
    h = pl.program_id(0)
    cols = o_ref.shape[2]
    r = lax.broadcasted_iota(I32, (rb, cols), 0) + pl.program_id(1) * rb
    c = lax.broadcasted_iota(I32, (rb, cols), 1)
    d = ar * r + ac * c + c0
    v = _bias_from_dist(d, tbl_ref, hoff + h)
    if mult != 1.0:
        v = v * mult
    ok = jnp.where(d >= 0, jnp.where(d <= dmax, jnp.where(c < cmax, 1, 0), 0), 0)
    o_ref[0] = jnp.where(ok > 0, v, NEG)


def _bias_strip(tbl, hoff, nh, rows, cols, ar, ac, c0, dmax=2 ** 30, cmax=2 ** 30, mult=1.0):
    rb = rows if rows <= 256 else 256
    assert rows % rb == 0
    return pl.pallas_call(
        functools.partial(_strip_body, hoff=hoff, ar=ar, ac=ac, c0=c0, dmax=dmax, cmax=cmax, rb=rb, mult=mult),
        grid=(nh, rows // rb),
        in_specs=[pl.BlockSpec(memory_space=pltpu.SMEM)],
        out_specs=pl.BlockSpec((1, rb, cols), lambda h, r: (h, r, 0)),
        out_shape=jax.ShapeDtypeStruct((nh, rows, cols), F32),
        compiler_params=_cparams(("arbitrary", "arbitrary")),
        name="bias_strip",
    )(tbl)


def _mm_body(*refs, norm, epi, gsize):
    refs = list(refs)
    x_ref = refs.pop(0)
    g_ref = refs.pop(0) if norm else None
    w_ref = refs.pop(0)
    if epi == "grms":
        ind_ref, indt_ref, gain_ref = refs.pop(0), refs.pop(0), refs.pop(0)
    o_ref, xn_ref = refs

    @pl.when(pl.program_id(1) == 0)
    def _():
        x = x_ref[...]
        if norm:
            x = _rms(x, g_ref[...])
        xn_ref[...] = x.astype(BF16)

    y = jnp.dot(xn_ref[...], w_ref[...], preferred_element_type=F32)
    if epi == "sigmoid":
        y = jax.nn.sigmoid(y)
    elif epi == "grms":
        y = _group_rms(y, ind_ref[...], indt_ref[...], gain_ref[...], gsize)
    o_ref[...] = y.astype(o_ref.dtype)


def _mm(x, w, *, g=None, epi=None, gsize=1, gain=None, out_dtype=F32, tm=512, tn=None):
    m, k = x.shape
    n = w.shape[1]
    tm = min(tm, m)
    tn = n if (tn is None or epi == "grms") else tn
    assert m % tm == 0 and n % tn == 0
    args = [x]
    specs = [pl.BlockSpec((tm, k), lambda i, j: (i, 0))]
    if g is not None:
        args.append(g.reshape(1, k).astype(F32))
        specs.append(pl.BlockSpec((1, k), lambda i, j: (0, 0)))
    args.append(w.astype(BF16))
    specs.append(pl.BlockSpec((k, tn), lambda i, j: (0, j)))
    if epi == "grms":
        ind, ind_t = _group_indicator(n, gsize)
        args += [ind, ind_t, gain.reshape(1, n).astype(F32)]
        specs += [pl.BlockSpec((n, LANES), lambda i, j: (0, 0)),
                  pl.BlockSpec((LANES, n), lambda i, j: (0, 0)),
                  pl.BlockSpec((1, n), lambda i, j: (0, 0))]
    return pl.pallas_call(
        functools.partial(_mm_body, norm=g is not None, epi=epi, gsize=gsize),
        grid=(m // tm, n // tn),
        in_specs=specs,
        out_specs=pl.BlockSpec((tm, tn), lambda i, j: (i, j)),
        out_shape=jax.ShapeDtypeStruct((m, n), out_dtype),
        scratch_shapes=[pltpu.VMEM((tm, k), BF16)],
        compiler_params=_cparams(("arbitrary", "arbitrary")),
        name="norm_matmul",
    )(*args)


FA_NQ, FA_CMP, FA_SEL, FA_WIN, FA_CQ, FA_CKV, FA_DQ, FA_DK, FA_DV, FA_NG, FA_KR, FA_END = (
    0, 512, 640, 768, 896, 1152, 1280, 1792, 1920, 2048, 2176, 2304)
GN_NQ, GN_KS, GN_KW, GN_CQ, GN_CKV, GN_QN, GN_QR, GN_KR, GN_DQ, GN_DK, GN_END = (
    0, 512, 640, 768, 1024, 1152, 1664, 1920, 2048, 2560, 2688)


def _half_rms(x, g, lo_half):
    lane = lax.broadcasted_iota(I32, x.shape, 1)
    sel = (lane < NSA_DH) if lo_half else (lane >= NSA_DH)
    ssq = jnp.sum(jnp.where(sel, x * x, 0.0), axis=-1, keepdims=True)
    return jnp.where(sel, x * lax.rsqrt(ssq * (1.0 / NSA_DH) + EPS) * g, x)


def _feat_body(x_ref, gmix_ref, wa_ref, wuq_ref, cos_ref, sin_ref, gains_ref,
               i512_ref, i512t_ref, i16_ref, i16t_ref,
               cmp_ref, sel_ref, win_ref, mla_ref, diff_ref,
               nq_ref, gate_ref, qn_ref, qr_ref, dq_ref):
    gains = gains_ref[...]
    gslice = lambda a, b: gains[:, a:b]
    xn = _rms(x_ref[...], gmix_ref[...]).astype(BF16)
    h = jnp.dot(xn, wa_ref[...], preferred_element_type=F32)
    i512, i512t = i512_ref[...], i512t_ref[...]
    i16, i16t = i16_ref[...], i16t_ref[...]
    cos, sin = cos_ref[...], sin_ref[...]
    half = MLA_ROPE // 2

    nq_ref[...] = _group_rms(h[:, FA_NQ:FA_CMP], i512, i512t, gslice(GN_NQ, GN_KS), NSA_DH)
    cmp_ref[...] = h[:, FA_CMP:FA_SEL]
    sel_ref[...] = _half_rms(h[:, FA_SEL:FA_WIN], gslice(GN_KS, GN_KW), True)
    win_ref[...] = _half_rms(h[:, FA_WIN:FA_CQ], gslice(GN_KW, GN_CQ), True)
    gate_ref[...] = jax.nn.sigmoid(h[:, FA_NG:FA_KR])

    cqn = _rms(h[:, FA_CQ:FA_CKV], gslice(GN_CQ, GN_CKV)).astype(BF16)
    q = jnp.dot(cqn, wuq_ref[...], preferred_element_type=F32)
    nope_w = MLA_HEADS * MLA_NOPE
    qn_ref[...] = _group_rms(q[:, :nope_w], i512, i512t, gslice(GN_QN, GN_QR), MLA_NOPE)
    x1 = q[:, nope_w:nope_w + LANES]
    x2 = q[:, nope_w + LANES:nope_w + 2 * LANES]
    rs = lax.rsqrt(_dot_hl(x1 * x1 + x2 * x2, i16) * (1.0 / MLA_ROPE) + EPS)
    sc = _dot_hl(rs, i16t)
    x1 = x1 * sc * gslice(GN_QR, GN_QR + LANES)
    x2 = x2 * sc * gslice(GN_QR + LANES, GN_KR)
    qr_ref[:, 0:LANES] = x1 * cos - x2 * sin
    qr_ref[:, LANES:2 * LANES] = x1 * sin + x2 * cos
    mla_ref[:, 0:MLA_KV_RANK] = _rms(h[:, FA_CKV:FA_DQ], gslice(GN_CKV, GN_QN))
    kr = h[:, FA_KR:FA_END]
    krn = kr * lax.rsqrt(jnp.sum(kr * kr, axis=-1, keepdims=True) * (1.0 / MLA_ROPE) + EPS) * gslice(GN_KR, GN_DQ)
    k1, k2 = krn[:, 0:half], krn[:, half:MLA_ROPE]
    c16, s16 = cos[:, 0:half], sin[:, 0:half]
    mla_ref[:, MLA_KV_RANK:MLA_KV_RANK + MLA_ROPE] = jnp.concatenate(
        [k1 * c16 - k2 * s16, k1 * s16 + k2 * c16], axis=-1)

    dq_ref[...] = _group_rms(h[:, FA_DQ:FA_DK], i512, i512t, gslice(GN_DQ, GN_DK), DIFF_DH)
    dk = h[:, FA_DK:FA_DV]
    gdk = gslice(GN_DK, GN_END)
    dkn = _half_rms(_half_rms(dk, gdk, True), gdk, False)
    diff_ref[:, 0:2 * DIFF_DH] = dkn
    diff_ref[:, 2 * DIFF_DH:4 * DIFF_DH] = h[:, FA_DV:FA_NG]


def _features(x, pos, lw, tm=256):
    t, d = x.shape
    tm = min(tm, t)
    assert t % tm == 0
    half = MLA_ROPE // 2
    inv = ROPE_BASE ** (-jnp.arange(half, dtype=F32) / half)
    ang = pos.astype(F32)[:, None] * inv
    cos = jnp.tile(jnp.cos(ang), (1, LANES // half))
    sin = jnp.tile(jnp.sin(ang), (1, LANES // half))
    i512, i512t = _group_indicator(512, 64)
    i16, i16t = _group_indicator(LANES, half)
    row = lambda w: pl.BlockSpec((tm, w), lambda i: (i, 0))
    full = lambda a: pl.BlockSpec(a.shape, lambda i: (0,) * a.ndim)
    consts = [lw["g_mix"], lw["w_a"], lw["w_uq"]]
    tail = [lw["gains"], i512, i512t, i16, i16t]
    outs = [(128, "cmp"), (128, "sel"), (128, "win"), (MLA_KV_RANK + MLA_ROPE, "mla"), (256, "diff"),
            (512, "nq"), (128, "gate"), (512, "qn"), (256, "qr"), (512, "dq")]
    res = pl.pallas_call(
        _feat_body,
        grid=(t // tm,),
        in_specs=[row(d)] + [full(a) for a in consts] + [row(LANES), row(LANES)] + [full(a) for a in tail],
        out_specs=[row(w) for w, _ in outs],
        out_shape=[jax.ShapeDtypeStruct((t, w), F32) for w, _ in outs],
        compiler_params=_cparams(("arbitrary",)),
        name="mixer_features",
    )(x, *consts, cos, sin, *tail)
    return {name: r for (_, name), r in zip(outs, res)}


def _blockmean_body(pt_ref, *refs, npg, page):
    del pt_ref
    pages, g_ref, o_ref = refs[:npg], refs[npg], refs[npg + 1]
    per = page // CMP_BLOCK
    for p in range(npg):
        pg = pages[p][0, 0]
        o_ref[0, p * per:(p + 1) * per, :] = jnp.mean(pg.reshape(per, CMP_BLOCK, pg.shape[-1]), axis=1)
    o_ref[0] = _half_rms(o_ref[0], g_ref[...], True)


def _block_means(pool, layer, page_table, g_cmp):
    nb, n_pages = page_table.shape
    page, w = pool.shape[2], pool.shape[3]
    npg = min(DEC_PAGES, n_pages)
    assert n_pages % npg == 0
    per = page // CMP_BLOCK

    def pspec(p):
        return pl.BlockSpec((1, 1, page, w), lambda b, c, pt: (layer, pt[b, c * npg + p], 0, 0))

    gain = jnp.concatenate([g_cmp.astype(F32), jnp.ones((NSA_DH,), F32)]).reshape(1, 2 * NSA_DH)
    return pl.pallas_call(
        functools.partial(_blockmean_body, npg=npg, page=page),
        grid_spec=pltpu.PrefetchScalarGridSpec(
            num_scalar_prefetch=1,
            grid=(nb, n_pages // npg),
            in_specs=[pspec(p) for p in range(npg)] + [pl.BlockSpec((1, w), lambda b, c, pt: (0, 0))],
            out_specs=pl.BlockSpec((1, npg * per, w), lambda b, c, pt: (b, c, 0)),
        ),
        out_shape=jax.ShapeDtypeStruct((nb, n_pages * per, w), F32),
        compiler_params=_cparams(("arbitrary", "arbitrary")),
        name="block_means",
    )(page_table, *([pool] * npg), gain)


def _cmp_body(tbl_ref, q_ref, kc_ref, vc_ref, o_ref, sel_ref, *, nh, tq, qbase, n_cmp, add_cur):
    ncol = kc_ref.shape[1]
    qpos = qbase + pl.program_id(1) * tq + lax.broadcasted_iota(I32, (tq, ncol), 0)
    col = lax.broadcasted_iota(I32, (tq, ncol), 1)
    cur = qpos // CMP_BLOCK
    vis = jnp.where(col < cur, jnp.where(col < n_cmp, 1.0, 0.0), 0.0)
    dist = qpos - (col * CMP_BLOCK + CMP_BLOCK - 1)
    psum = jnp.zeros((tq, ncol), F32)
    for h in range(nh):
        s = lax.dot_general(q_ref[0, h], kc_ref[0], NT_DIMS, preferred_element_type=F32)
        s = jnp.where(vis > 0, s + _bias_from_dist(dist, tbl_ref, h), NEG)
        p = jnp.exp(s - jnp.max(s, axis=-1, keepdims=True)) * vis
        p = p / jnp.maximum(jnp.sum(p, axis=-1, keepdims=True), 1e-30)
        o_ref[0, h] = jnp.dot(p.astype(BF16), vc_ref[0], preferred_element_type=F32)
        psum = psum + p

    score = jnp.where(vis > 0, psum, -1.0)
    chosen = jnp.zeros((tq, ncol), F32)
    for _ in range(min(N_SEL - 1, n_cmp)):
        m = jnp.max(score, axis=-1, keepdims=True)
        first = jnp.min(jnp.where(score == m, col, ncol), axis=-1, keepdims=True)
        hit = col == first
        chosen = jnp.where(hit, jnp.where(m >= 0.0, 1.0, 0.0), chosen)
        score = jnp.where(hit, -2.0, score)
    if add_cur:
        chosen = jnp.where(col == cur, 1.0, chosen)
    sel_ref[0] = chosen.astype(BF16)


def _cmp_attention(tbl, q, kc, vc, *, tq, qbase, n_cmp, add_cur):
    nb, nh, t, _ = q.shape
    ncol = kc.shape[1]
    return pl.pallas_call(
        functools.partial(_cmp_body, nh=nh, tq=tq, qbase=qbase, n_cmp=n_cmp, add_cur=add_cur),
        grid=(nb, t // tq),
        in_specs=[pl.BlockSpec(memory_space=pltpu.SMEM),
                  pl.BlockSpec((1, nh, tq, NSA_DH), lambda b, i: (b, 0, i, 0)),
                  pl.BlockSpec((1, ncol, NSA_DH), lambda b, i: (b, 0, 0)),
                  pl.BlockSpec((1, ncol, NSA_DH), lambda b, i: (b, 0, 0))],
        out_specs=[pl.BlockSpec((1, nh, tq, NSA_DH), lambda b, i: (b, 0, i, 0)),
                   pl.BlockSpec((1, tq, ncol), lambda b, i: (b, i, 0))],
        out_shape=[jax.ShapeDtypeStruct((nb, nh, t, NSA_DH), F32),
                   jax.ShapeDtypeStruct((nb, t, ncol), BF16)],
        compiler_params=_cparams(("arbitrary", "arbitrary")),
        name="cmp_attention",
    )(tbl, q, kc, vc)


def _flash_body(qi_ref, kj_ref, fl_ref, *refs, ng, nh, gv, tq, tkb, scale, dclamp, has_qadd):
    refs = list(refs)
    q_ref, k_ref, v_ref, strip_ref = refs[:4]
    refs = refs[4:]
    if has_qadd:
        qadd_ref = refs.pop(0)
    o_ref, m_ref, l_ref, acc_ref = refs
    step = pl.program_id(0)
    flags = fl_ref[step]
    delta0 = qi_ref[step] * tq - kj_ref[step] * tkb
    ch = min(FLASH_CHUNK, tkb)
    dv = o_ref.shape[-1]
    dvp = acc_ref.shape[-1]
    ones_col = dvp > dv

    @pl.when(flags % 2 == 1)
    def _():
        _init_softmax_state(m_ref, l_ref, acc_ref)

    for g in range(ng):
        q = q_ref[g]
        if has_qadd:
            q = q + qadd_ref[0][None]
        q = q.reshape(nh * tq, q_ref.shape[-1])
        chunks = []
        m_cur = None
        for c in range(tkb // ch):
            s = lax.dot_general(q, k_ref[g, c * ch:(c + 1) * ch, :], NT_DIMS, preferred_element_type=F32)
            if scale != 1.0:
                s = s * scale
            slabs = []
            for u in range(ch // LANES):
                delta = delta0 - (c * ch + u * LANES)
                r0 = pl.multiple_of(jnp.clip(delta, -tq, dclamp) + tq, SUBLANES)
                slabs.append(strip_ref[:, pl.ds(r0, tq), :])
            bias = slabs[0] if len(slabs) == 1 else jnp.concatenate(slabs, axis=-1)
            s = (s.reshape(nh, tq, ch) + bias).reshape(nh * tq, ch)
            chunks.append(s)
            mc = jnp.max(s, axis=-1, keepdims=True)
            m_cur = mc if m_cur is None else jnp.maximum(m_cur, mc)
        m_prev = m_ref[g]
        m_new = jnp.maximum(m_prev, m_cur)
        alpha = jnp.exp2(m_prev - m_new)
        m_b = pltpu.repeat(m_new, ch // LANES, axis=1)
        ps = []
        l_add = None
        for s in chunks:
            p = jnp.exp2(s - m_b)
            if not ones_col:
                ls = jnp.sum(p, axis=-1, keepdims=True)
                l_add = ls if l_add is None else l_add + ls
            ps.append(p.astype(BF16))
        p_all = ps[0] if len(ps) == 1 else jnp.concatenate(ps, axis=-1)
        pv = jnp.dot(p_all, v_ref[g if gv > 1 else 0], preferred_element_type=F32)
        if not ones_col:
            l_ref[g] = l_ref[g] * alpha + l_add
        acc_ref[g] = acc_ref[g] * (alpha if dvp == LANES else pltpu.repeat(alpha, dvp // LANES, axis=1)) + pv
        m_ref[g] = m_new

    @pl.when(flags >= 2)
    def _():
        for g in range(ng):
            acc = acc_ref[g]
            den = acc[:, dv:dv + 1] if ones_col else l_ref[g][:, 0:1]
            o_ref[g] = (acc[:, :dv] / den).reshape(nh, tq, dv)


def _flash_pairs(t, tq, tkb, window):
    qi, kj, fl = [], [], []
    for i in range(t // tq):
        q0 = i * tq
        jmax = (q0 + tq - 1) // tkb
        jmin = 0 if window is None else max(0, (q0 - (window - 1)) // tkb)
        for j in range(jmin, jmax + 1):
            qi.append(i)
            kj.append(j)
            fl.append((1 if j == jmin else 0) + (2 if j == jmax else 0))
    return (jnp.asarray(np.array(a, np.int32)) for a in (qi, kj, fl))


def _with_ones_column(v):
    pad = jnp.zeros(v.shape[:-1] + (LANES,), v.dtype).at[..., 0].set(1)
    return jnp.concatenate([v, pad], axis=-1) if v.shape[-1] % LANES == 0 else jnp.concatenate(
        [v, pad[..., :LANES - v.shape[-1] % LANES]], axis=-1)


def _flash(q, k, v, strip, *, tq, tkb, scale, dclamp, window=None, qadd=None):
    ng, nh, t, d = q.shape
    gv, _, dv = v.shape
    tq, tkb = min(tq, t), min(tkb, t)
    assert t % tq == 0 and t % tkb == 0 and tq % LANES == 0 and tkb % min(FLASH_CHUNK, tkb) == 0
    assert strip.shape[2] == LANES and strip.shape[1] >= dclamp + 2 * tq
    v1 = _with_ones_column(v) if dv < LANES else v
    dvp = v1.shape[-1]
    qi, kj, fl = _flash_pairs(t, tq, tkb, window)
    args = [q, k, v1, strip]
    specs = [pl.BlockSpec((ng, nh, tq, d), lambda s, qi, kj, fl: (0, 0, qi[s], 0)),
             pl.BlockSpec((ng, tkb, d), lambda s, qi, kj, fl: (0, kj[s], 0)),
             pl.BlockSpec((gv, tkb, dvp), lambda s, qi, kj, fl: (0, kj[s], 0)),
             pl.BlockSpec(strip.shape, lambda s, qi, kj, fl: (0, 0, 0))]
    if qadd is not None:
        args.append(qadd)
        specs.append(pl.BlockSpec((1, tq, d), lambda s, qi, kj, fl: (kj[s], qi[s], 0)))
    return pl.pallas_call(
        functools.partial(_flash_body, ng=ng, nh=nh, gv=gv, tq=tq, tkb=tkb, scale=scale,
                          dclamp=dclamp, has_qadd=qadd is not None),
        grid_spec=pltpu.PrefetchScalarGridSpec(
            num_scalar_prefetch=3,
            grid=(qi.shape[0],),
            in_specs=specs,
            out_specs=pl.BlockSpec((ng, nh, tq, dv), lambda s, qi, kj, fl: (0, 0, qi[s], 0)),
            scratch_shapes=[pltpu.VMEM((ng, nh * tq, LANES), F32),
                            pltpu.VMEM((ng, nh * tq, LANES) if dvp == dv else (1, SUBLANES, LANES), F32),
                            pltpu.VMEM((ng, nh * tq, dvp), F32)],
        ),
        out_shape=jax.ShapeDtypeStruct((ng, nh, t, dv), F32),
        compiler_params=_cparams(("arbitrary",)),
        name="prompt_attention",
    )(qi, kj, fl, *args)


def _decode_body(pt_ref, *refs, npg, page, kw, vlo, vhi, nh, has_sel):
    del pt_ref
    refs = list(refs)
    q_ref, bias_ref = refs[:2]
    refs = refs[2:]
    if has_sel:
        sel_ref, e_ref = refs[:2]
        refs = refs[2:]
    new_ref, nbias_ref = refs[:2]
    pages = refs[2:2 + npg]
    o_ref, cb_ref, m_ref, l_ref, acc_ref = refs[2 + npg:]
    c = pl.program_id(1)
    dv = vhi - vlo

    @pl.when(c == 0)
    def _():
        _init_softmax_state(m_ref, l_ref, acc_ref)

    for p in range(npg):
        cb_ref[p * page:(p + 1) * page, :] = pages[p][0, 0].astype(BF16)
    q = q_ref[0]
    ns = m_ref.shape[0]
    hk = npg * page // ns
    for h in range(ns):
        lo, hi = h * hk, (h + 1) * hk
        s = lax.dot_general(q, cb_ref[lo:hi, 0:kw], NT_DIMS, preferred_element_type=F32) + bias_ref[:, lo:hi]
        if has_sel:
            picked = jnp.dot(sel_ref[0], e_ref[:, lo:hi], preferred_element_type=F32)
            nrow = s.shape[0]
            s = (s.reshape(nh, nrow // nh, hk) + ((picked - 1.0) * (-NEG))[None]).reshape(nrow, hk)
        p, alpha = _softmax_step(s, m_ref.at[h], l_ref.at[h])
        acc_ref[h] = acc_ref[h] * alpha[:, :dv] + jnp.dot(p.astype(BF16), cb_ref[lo:hi, vlo:vhi],
                                                          preferred_element_type=F32)

    @pl.when(c == pl.num_programs(1) - 1)
    def _():
        nw = new_ref[0].astype(BF16)
        s2 = lax.dot_general(q, nw[:, 0:kw], NT_DIMS, preferred_element_type=F32) + nbias_ref[...]
        p2, alpha2 = _softmax_step(s2, m_ref.at[0], l_ref.at[0])
        acc_ref[0] = acc_ref[0] * alpha2[:, :dv] + jnp.dot(p2.astype(BF16), nw[:, vlo:vhi],
                                                           preferred_element_type=F32)
        o_ref[0] = _merge_softmax_states(m_ref, l_ref, acc_ref)


def _decode_attention(pool, layer, page_table, q, bias, new_rows, new_bias, *, kw, vlo, vhi, nh,
                      sel=None, expand=None):
    nb, n_pages = page_table.shape
    page, w = pool.shape[2], pool.shape[3]
    r = q.shape[1]
    npg = min(DEC_PAGES, n_pages)
    assert n_pages % npg == 0
    ck = npg * page

    def pspec(p):
        return pl.BlockSpec((1, 1, page, w), lambda b, c, pt: (layer, pt[b, c * npg + p], 0, 0))

    args = [q, bias]
    specs = [pl.BlockSpec((1, r, kw), lambda b, c, pt: (b, 0, 0)),
             pl.BlockSpec((r, ck), lambda b, c, pt: (0, c))]
    if sel is not None:
        args += [sel, expand]
        specs += [pl.BlockSpec((1, DEC_Q, sel.shape[2]), lambda b, c, pt: (b, 0, 0)),
                  pl.BlockSpec((expand.shape[0], ck), lambda b, c, pt: (0, c))]
    args += [new_rows, new_bias]
    specs += [pl.BlockSpec((1, NEW_PAD, w), lambda b, c, pt: (b, 0, 0)),
              pl.BlockSpec((r, NEW_PAD), lambda b, c, pt: (0, 0))]
    args += [pool] * npg
    specs += [pspec(p) for p in range(npg)]
    dv = vhi - vlo
    return pl.pallas_call(
        functools.partial(_decode_body, npg=npg, page=page, kw=kw, vlo=vlo, vhi=vhi, nh=nh,
                          has_sel=sel is not None),
        grid_spec=pltpu.PrefetchScalarGridSpec(
            num_scalar_prefetch=1,
            grid=(nb, n_pages // npg),
            in_specs=specs,
            out_specs=pl.BlockSpec((1, r, dv), lambda b, c, pt: (b, 0, 0)),
            scratch_shapes=[pltpu.VMEM((ck, w), BF16), pltpu.VMEM((DEC_SPLIT, r, LANES), F32),
                            pltpu.VMEM((DEC_SPLIT, r, LANES), F32), pltpu.VMEM((DEC_SPLIT, r, dv), F32)],
        ),
        out_shape=jax.ShapeDtypeStruct((nb, r, dv), F32),
        compiler_params=_cparams(("arbitrary", "arbitrary")),
        name="decode_attention",
    )(page_table, *args)


def _mla_decode_body(pt_ref, qbd_ref, qr_ref, wuk_ref, wuv_ref, new_ref, nbias_ref, *refs, npg, page):
    del pt_ref
    pages = refs[:npg]
    o_ref, cb_ref, w_ref, m_ref, l_ref, acc_ref = refs[npg:]
    c = pl.program_id(1)
    scale = (MLA_NOPE + MLA_ROPE) ** -0.5
    nrow = qbd_ref.shape[1]
    per = nrow // MLA_HEADS
    nk = MLA_HEADS * MLA_NOPE

    @pl.when(c == 0)
    def _():
        _init_softmax_state(m_ref, l_ref, acc_ref)
        w_ref[...] = jnp.zeros(w_ref.shape, BF16)
        w_ref[0:nk, 0:MLA_KV_RANK] = wuk_ref[...]
        absorbed = jnp.dot(qbd_ref[0], wuk_ref[...], preferred_element_type=F32)
        w_ref[nk:nk + nrow, 0:MLA_KV_RANK] = absorbed.astype(BF16)
        w_ref[nk + nrow:nk + 2 * nrow, MLA_KV_RANK:MLA_KV_RANK + MLA_ROPE] = qr_ref[0]

    def scores(feat_t):
        n = feat_t.shape[1]
        big = jnp.dot(w_ref[...], feat_t, preferred_element_type=F32)
        knt = big[0:nk]
        ssq = jnp.sum((knt * knt).reshape(MLA_HEADS, MLA_NOPE, n), axis=1)
        rs = lax.rsqrt(ssq * (1.0 / MLA_NOPE) + EPS)
        sn = (big[nk:nk + nrow].reshape(MLA_HEADS, per, n) * rs[:, None, :]).reshape(nrow, n)
        return (sn + big[nk + nrow:nk + 2 * nrow]) * scale

    def accumulate(s, feat_t, h):
        p, alpha = _softmax_step(s, m_ref.at[h], l_ref.at[h])
        ctx = lax.dot_general(p.astype(BF16), feat_t[0:MLA_KV_RANK], NT_DIMS, preferred_element_type=F32)
        acc_ref[h] = acc_ref[h] * alpha + ctx

    for p in range(npg):
        cb_ref[:, p * page:(p + 1) * page] = pages[p][0, 0].astype(BF16)
    ns = m_ref.shape[0]
    hk = npg * page // ns
    for h in range(ns):
        feat = cb_ref[:, h * hk:(h + 1) * hk]
        accumulate(scores(feat), feat, h)

    @pl.when(c == pl.num_programs(1) - 1)
    def _():
        nw = new_ref[0].astype(BF16)
        accumulate(scores(nw) + nbias_ref[...], nw, 0)
        ctx = _merge_softmax_states(m_ref, l_ref, acc_ref).astype(BF16)
        full = jnp.dot(ctx, wuv_ref[...], preferred_element_type=F32)
        for h in range(MLA_HEADS):
            o_ref[0, h * per:(h + 1) * per, :] = full[h * per:(h + 1) * per, h * MLA_V:(h + 1) * MLA_V]


def _mla_decode(pool_t, layer, page_table, qbd, qr, wuk_t, wuv, new_rows_t, new_bias):
    nb, n_pages = page_table.shape
    w, page = pool_t.shape[2], pool_t.shape[3]
    r = qbd.shape[1]
    npg = min(MLA_DEC_PAGES, n_pages)
    assert n_pages % npg == 0 and page == LANES

    def pspec(p):
        return pl.BlockSpec((1, 1, w, page), lambda b, c, pt: (layer, pt[b, c * npg + p], 0, 0))

    full = lambda a: pl.BlockSpec(a.shape, lambda b, c, pt: (0,) * a.ndim)
    return pl.pallas_call(
        functools.partial(_mla_decode_body, npg=npg, page=page),
        grid_spec=pltpu.PrefetchScalarGridSpec(
            num_scalar_prefetch=1,
            grid=(nb, n_pages // npg),
            in_specs=[pl.BlockSpec((1, r, qbd.shape[2]), lambda b, c, pt: (b, 0, 0)),
                      pl.BlockSpec((1, r, MLA_ROPE), lambda b, c, pt: (b, 0, 0)),
                      full(wuk_t), full(wuv),
                      pl.BlockSpec((1, w, NEW_PAD), lambda b, c, pt: (b, 0, 0)),
                      full(new_bias)] + [pspec(p) for p in range(npg)],
            out_specs=pl.BlockSpec((1, r, MLA_V), lambda b, c, pt: (b, 0, 0)),
            scratch_shapes=[pltpu.VMEM((w, npg * page), BF16),
                            pltpu.VMEM((MLA_HEADS * MLA_NOPE + 2 * r, w), BF16),
                            pltpu.VMEM((DEC_SPLIT, r, LANES), F32), pltpu.VMEM((DEC_SPLIT, r, LANES), F32),
                            pltpu.VMEM((DEC_SPLIT, r, MLA_KV_RANK), F32)],
        ),
        out_shape=jax.ShapeDtypeStruct((nb, r, MLA_V), F32),
        compiler_params=_cparams(("arbitrary", "arbitrary")),
        name="mla_decode",
    )(page_table, qbd, qr, wuk_t, wuv, new_rows_t, new_bias, *([pool_t] * npg))


def _merge_body(lam_ref, x_ref, ocmp_ref, osel_ref, owin_ref, gate_ref, omla_ref, odiff_ref, subln_ref,
                mg_ref, wb_ref, wo_ref, o_ref, *, post_scale):
    d = x_ref.shape[1]
    gate = gate_ref[...]
    nsa = []
    for h in range(NSA_HEADS):
        nsa.append(gate[:, 3 * h:3 * h + 1] * ocmp_ref[h] + gate[:, 3 * h + 1:3 * h + 2] * osel_ref[h]
                   + gate[:, 3 * h + 2:3 * h + 3] * owin_ref[h])
    br_nsa = jnp.concatenate(nsa, axis=-1)
    br_mla = jnp.concatenate([omla_ref[h] for h in range(MLA_HEADS)], axis=-1)
    lam = lam_ref[0]
    dif = []
    for h in range(DIFF_HEADS):
        a = odiff_ref[0, h] - lam * odiff_ref[1, h]
        dif.append(_rms(a, subln_ref[...]) * post_scale)
    br_diff = jnp.concatenate(dif, axis=-1)
    acc = None
    for b, br in enumerate((br_nsa, br_mla, br_diff)):
        term = mg_ref[:, b * d:(b + 1) * d] * jnp.dot(br.astype(BF16), wb_ref[b], preferred_element_type=F32)
        acc = term if acc is None else acc + term
    o_ref[...] = x_ref[...] + jnp.dot(acc.astype(BF16), wo_ref[...], preferred_element_type=F32)


def _merge(x, o_cmp, o_sel, o_win, gate, o_mla, o_diff, lam, subln, mg, wb, wo, post_scale, tm=256):
    t, d = x.shape
    tm = min(tm, t)
    heads = lambda a: pl.BlockSpec((a.shape[0], tm, a.shape[2]), lambda i: (0, i, 0))
    row = lambda w: pl.BlockSpec((tm, w), lambda i: (i, 0))
    full = lambda a: pl.BlockSpec(a.shape, lambda i: (0,) * a.ndim)
    return pl.pallas_call(
        functools.partial(_merge_body, post_scale=post_scale),
        grid=(t // tm,),
        in_specs=[pl.BlockSpec(memory_space=pltpu.SMEM), row(d), heads(o_cmp), heads(o_sel), heads(o_win),
                  row(LANES), heads(o_mla),
                  pl.BlockSpec((2, DIFF_HEADS, tm, 2 * DIFF_DH), lambda i: (0, 0, i, 0)),
                  full(subln), row(3 * d), full(wb), full(wo)],
        out_specs=row(d),
        out_shape=jax.ShapeDtypeStruct((t, d), F32),
        compiler_params=_cparams(("arbitrary",)),
        name="merge_branches",
    )(lam, x, o_cmp, o_sel, o_win, gate, o_mla, o_diff, subln, mg, wb, wo)


def _xattn_body(x_ref, g_ref, wq_ref, gq_ref, k_ref, v_ref, wo_ref, o_ref):
    x = x_ref[0]
    xn = _rms(x, g_ref[...]).astype(BF16)
    acc = x
    for h in range(MEM_HEADS):
        q = _rms(jnp.dot(xn, wq_ref[h], preferred_element_type=F32), gq_ref[...]).astype(BF16)
        s = lax.dot_general(q, k_ref[0, h], NT_DIMS, preferred_element_type=F32) * MEM_DH ** -0.5
        p = jnp.exp(s - jnp.max(s, axis=-1, keepdims=True))
        p = p / jnp.sum(p, axis=-1, keepdims=True)
        o = jnp.dot(p.astype(BF16), v_ref[0, h], preferred_element_type=F32).astype(BF16)
        acc = acc + jnp.dot(o, wo_ref[h], preferred_element_type=F32)
    o_ref[0] = acc


def _cross_attend(x3, g, wq, gq, k4, v4, wo):
    nbk = k4.shape[0]
    nb, tm, d = x3.shape
    kidx = (lambda i: (i, 0, 0, 0)) if nbk > 1 else (lambda i: (0, 0, 0, 0))
    full = lambda a: pl.BlockSpec(a.shape, lambda i: (0,) * a.ndim)
    return pl.pallas_call(
        _xattn_body,
        grid=(nb,),
        in_specs=[pl.BlockSpec((1, tm, d), lambda i: (i, 0, 0)), full(g), full(wq), full(gq),
                  pl.BlockSpec((1,) + k4.shape[1:], kidx), pl.BlockSpec((1,) + v4.shape[1:], kidx), full(wo)],
        out_specs=pl.BlockSpec((1, tm, d), lambda i: (i, 0, 0)),
        out_shape=jax.ShapeDtypeStruct(x3.shape, F32),
        compiler_params=_cparams(("arbitrary",)),
        name="memory_cross_attention",
    )(x3, g, wq, gq, k4, v4, wo)


def _ffn_body(x_ref, g_ref, wg_ref, wu_ref, wd_ref, o_ref, xn_ref, acc_ref):
    f = pl.program_id(1)

    @pl.when(f == 0)
    def _():
        xn_ref[...] = _rms(x_ref[...], g_ref[...]).astype(BF16)
        acc_ref[...] = jnp.zeros(acc_ref.shape, F32)

    xn = xn_ref[...]
    hid = (jax.nn.silu(jnp.dot(xn, wg_ref[...], preferred_element_type=F32))
           * jnp.dot(xn, wu_ref[...], preferred_element_type=F32))
    acc_ref[...] += jnp.dot(hid.astype(BF16), wd_ref[...], preferred_element_type=F32)

    @pl.when(f == pl.num_programs(1) - 1)
    def _():
        o_ref[...] = x_ref[...] + acc_ref[...]


def _ffn_tiles(t, ff):
    tm = min(512, t)
    tf = 896 if ff % 896 == 0 else ff
    return tm, tf


def _dense_ffn(x, g, wg, wu, wd):
    t, d = x.shape
    ff = wg.shape[1]
    tm, tf = _ffn_tiles(t, ff)
    return pl.pallas_call(
        _ffn_body,
        grid=(t // tm, ff // tf),
        in_specs=[pl.BlockSpec((tm, d), lambda i, f: (i, 0)), pl.BlockSpec((1, d), lambda i, f: (0, 0)),
                  pl.BlockSpec((d, tf), lambda i, f: (0, f)), pl.BlockSpec((d, tf), lambda i, f: (0, f)),
                  pl.BlockSpec((tf, d), lambda i, f: (f, 0))],
        out_specs=pl.BlockSpec((tm, d), lambda i, f: (i, 0)),
        out_shape=jax.ShapeDtypeStruct((t, d), F32),
        scratch_shapes=[pltpu.VMEM((tm, d), BF16), pltpu.VMEM((tm, d), F32)],
        compiler_params=_cparams(("arbitrary", "arbitrary")),
        name="dense_ffn",
    )(x, g, wg, wu, wd)


def _moe_body(x_ref, g_ref, wrh_ref, wrl_ref, wg_ref, wu_ref, wd_ref, o_ref, xn_ref, gate_ref, acc_ref):
    e = pl.program_id(1)
    f = pl.program_id(2)

    @pl.when((e == 0) & (f == 0))
    def _():
        xf = _rms(x_ref[...], g_ref[...])
        xh, xl = _split_hi_lo(xf)
        xn_ref[...] = xh
        acc_ref[...] = jnp.zeros(acc_ref.shape, F32)
        logits = (jnp.dot(xh, wrh_ref[...], preferred_element_type=F32)
                  + jnp.dot(xl, wrh_ref[...], preferred_element_type=F32)
                  + jnp.dot(xh, wrl_ref[...], preferred_element_type=F32))
        lane = lax.broadcasted_iota(I32, logits.shape, 1)
        logits = jnp.where(lane < N_EXPERTS, logits, -jnp.inf)
        m1 = jnp.max(logits, axis=-1, keepdims=True)
        i1 = jnp.min(jnp.where(logits == m1, lane, LANES), axis=-1, keepdims=True)
        rest = jnp.where(lane == i1, -jnp.inf, logits)
        m2 = jnp.max(rest, axis=-1, keepdims=True)
        i2 = jnp.min(jnp.where(rest == m2, lane, LANES), axis=-1, keepdims=True)
        e2 = jnp.exp(m2 - m1)
        w1 = 1.0 / (1.0 + e2)
        w2 = e2 / (1.0 + e2)
        gate_ref[...] = jnp.where(lane == i1, w1, 0.0) + jnp.where(lane == i2, w2, 0.0)

    xn = xn_ref[...]
    lane = lax.broadcasted_iota(I32, gate_ref.shape, 1)
    ge = jnp.sum(jnp.where(lane == e, gate_ref[...], 0.0), axis=-1, keepdims=True)
    hid = (jax.nn.silu(jnp.dot(xn, wg_ref[0], preferred_element_type=F32))
           * jnp.dot(xn, wu_ref[0], preferred_element_type=F32))
    acc_ref[...] += jnp.dot((hid * ge).astype(BF16), wd_ref[0], preferred_element_type=F32)

    @pl.when((e == pl.num_programs(1) - 1) & (f == pl.num_programs(2) - 1))
    def _():
        o_ref[...] = x_ref[...] + acc_ref[...]


def _moe_ffn(x, g, w_router, wg, wu, wd):
    t, d = x.shape
    ne, _, ff = wg.shape
    tm, tf = _ffn_tiles(t, ff)
    wr = jnp.pad(w_router.astype(F32), ((0, 0), (0, LANES - ne)))
    wrh = wr.astype(BF16)
    wrl = (wr - wrh.astype(F32)).astype(BF16)
    return pl.pallas_call(
        _moe_body,
        grid=(t // tm, ne, ff // tf),
        in_specs=[pl.BlockSpec((tm, d), lambda i, e, f: (i, 0)), pl.BlockSpec((1, d), lambda i, e, f: (0, 0)),
                  pl.BlockSpec((d, LANES), lambda i, e, f: (0, 0)), pl.BlockSpec((d, LANES), lambda i, e, f: (0, 0)),
                  pl.BlockSpec((1, d, tf), lambda i, e, f: (e, 0, f)),
                  pl.BlockSpec((1, d, tf), lambda i, e, f: (e, 0, f)),
                  pl.BlockSpec((1, tf, d), lambda i, e, f: (e, f, 0))],
        out_specs=pl.BlockSpec((tm, d), lambda i, e, f: (i, 0)),
        out_shape=jax.ShapeDtypeStruct((t, d), F32),
        scratch_shapes=[pltpu.VMEM((tm, d), BF16), pltpu.VMEM((tm, LANES), F32), pltpu.VMEM((tm, d), F32)],
        compiler_params=_cparams(("arbitrary", "arbitrary", "arbitrary")),
        name="moe_ffn",
    )(x, g, wrh, wrl, wg, wu, wd)


IN_SPLITS = (512, 384, 24, 256, 128, 32, 512, 128, 128, 3072)


def _layer_weights(l, p):
    d = p["w_in"].shape[1]
    w_in = p["w_in"][l]
    offs = np.cumsum((0,) + IN_SPLITS)
    seg = lambda i: w_in[:, offs[i]:offs[i + 1]]
    padl = lambda a, n: jnp.pad(a, ((0, 0), (0, n - a.shape[1])))
    w_a = jnp.concatenate([seg(0), seg(1), seg(3), seg(4), seg(6), seg(7), seg(8),
                           padl(seg(2), LANES), padl(seg(5), LANES)], axis=1).astype(BF16)
    half = MLA_ROPE // 2
    per = MLA_NOPE + MLA_ROPE
    wuq = p["w_mla_uq"][l].reshape(MLA_Q_RANK, MLA_HEADS, per)
    w_uq = jnp.concatenate([wuq[:, :, :MLA_NOPE].reshape(MLA_Q_RANK, -1),
                            wuq[:, :, MLA_NOPE:MLA_NOPE + half].reshape(MLA_Q_RANK, -1),
                            wuq[:, :, MLA_NOPE + half:].reshape(MLA_Q_RANK, -1)], axis=1).astype(BF16)
    gq, gk = p["mla_norm_q"][l], p["mla_norm_k"][l]
    ones64 = jnp.ones((NSA_DH,), F32)
    gains = jnp.concatenate([
        jnp.tile(p["nsa_q_norm"][l], NSA_HEADS),
        p["nsa_k_norm"][l, 1], ones64, p["nsa_k_norm"][l, 2], ones64,
        p["mla_q_rank_norm"][l], p["mla_kv_rank_norm"][l],
        jnp.tile(gq[:MLA_NOPE], MLA_HEADS),
        jnp.tile(gq[MLA_NOPE:MLA_NOPE + half], MLA_HEADS), jnp.tile(gq[MLA_NOPE + half:], MLA_HEADS),
        gk[MLA_NOPE:], jnp.zeros((LANES - MLA_ROPE,), F32),
        jnp.tile(p["diff_norm_q"][l].reshape(-1), DIFF_HEADS),
        p["diff_norm_k"][l].reshape(-1)]).astype(F32).reshape(1, GN_END)
    return {
        "g_mix": p["norm_mix"][l].reshape(1, d).astype(F32),
        "w_a": w_a, "w_uq": w_uq, "gains": gains,
        "w_mg": seg(9).astype(BF16),
        "gk_nope": gk[:MLA_NOPE].astype(F32),
        "w_uk": p["w_mla_uk"][l].astype(BF16), "w_uv": p["w_mla_uv"][l].astype(BF16),
        "g_cmp": p["nsa_k_norm"][l, 0],
        "subln": p["diff_subln"][l].reshape(1, -1).astype(F32),
        "wb": p["w_branch"][l].astype(BF16), "wo": p["w_out"][l].astype(BF16),
    }


def _heads_major(a, nh):
    t = a.shape[0]
    return a.reshape(t, nh, a.shape[1] // nh).transpose(1, 0, 2)


def _pad_cols(a, n):
    return jnp.pad(a, [(0, 0)] * (a.ndim - 1) + [(0, n - a.shape[-1])])


def _mla_q_heads(qn, qr):
    half = MLA_ROPE // 2
    t = qn.shape[0]
    parts = [qn.reshape(t, MLA_HEADS, MLA_NOPE), qr[:, :LANES].reshape(t, MLA_HEADS, half),
             qr[:, LANES:].reshape(t, MLA_HEADS, half)]
    return _pad_cols(jnp.concatenate(parts, axis=-1), LANES).transpose(1, 0, 2)


def kernel(x_prompt, x_sample, cache_nsa_cmp, cache_nsa_sel, cache_nsa_win, cache_mla, cache_diff, cache_mem, page_table, mem_prompt, rel_bias_table, norm_mix, w_in, nsa_q_norm, nsa_k_norm, mla_q_rank_norm, mla_kv_rank_norm, w_mla_uq, w_mla_uk, w_mla_uv, mla_norm_q, mla_norm_k, diff_norm_q, diff_norm_k, diff_lambda, diff_subln, w_branch, w_out, norm_mem_x, norm_mem_m, w_mem_q, w_mem_kv, mem_norm_qk, w_mem_o, norm_ffn, w_ffn_gate, w_ffn_up, w_ffn_down, w_router, w_exp_gate, w_exp_up, w_exp_down):
    prm = dict(w_in=w_in, norm_mix=norm_mix, nsa_q_norm=nsa_q_norm, nsa_k_norm=nsa_k_norm,
               mla_q_rank_norm=mla_q_rank_norm, mla_kv_rank_norm=mla_kv_rank_norm, w_mla_uq=w_mla_uq,
               w_mla_uk=w_mla_uk, w_mla_uv=w_mla_uv, mla_norm_q=mla_norm_q, mla_norm_k=mla_norm_k,
               diff_norm_q=diff_norm_q, diff_norm_k=diff_norm_k, diff_subln=diff_subln,
               w_branch=w_branch, w_out=w_out)
    bp, sp, d = x_prompt.shape
    db, ds, _ = x_sample.shape
    depth = w_in.shape[0]
    n_pages = page_table.shape[1]
    page = cache_nsa_cmp.shape[2]
    past = n_pages * page
    wb_len = cache_nsa_win.shape[2]
    n_mem = mem_prompt.shape[1]
    assert bp == 1 and ds <= DEC_Q and past % CMP_BLOCK == 0 and wb_len % page == 0
    half = MLA_ROPE // 2

    tbl = rel_bias_table.astype(F32)
    tbl0 = jnp.zeros_like(tbl)
    tq_a, tq_w, tq_m = min(ATT_TQ, sp), min(WIN_TQ, sp), min(MLA_TQ, sp)
    dclamp = _round_up(BIAS_FLAT_DIST + LANES - 1, SUBLANES)
    dclamp_w = _round_up(WINDOW - 1 + LANES, SUBLANES)

    def prompt_strip(table, hoff, nh, tq, dcl, dmax=2 ** 30):
        return _bias_strip(table, hoff, nh, _round_up(dcl + 2 * tq, 256), LANES, 1, -1, -tq, dmax=dmax, mult=LOG2E)

    s_sel = prompt_strip(tbl, 0, NSA_HEADS, tq_a, dclamp)
    s_diff = prompt_strip(tbl, NSA_HEADS, DIFF_HEADS, tq_a, dclamp)
    s_win = prompt_strip(tbl, 0, NSA_HEADS, tq_w, dclamp_w, dmax=WINDOW - 1)
    s_mla = prompt_strip(tbl0, 0, 1, tq_m, LANES)
    rows_hq = lambda a: a.reshape(a.shape[0] * a.shape[1], a.shape[2])
    b_sel = rows_hq(_bias_strip(tbl, 0, NSA_HEADS, DEC_Q, past, 1, -1, past))
    b_diff1 = _bias_strip(tbl, NSA_HEADS, DIFF_HEADS, DEC_Q, past, 1, -1, past)
    b_diff = rows_hq(jnp.concatenate([b_diff1, b_diff1], axis=0))
    b_win = rows_hq(_bias_strip(tbl, 0, NSA_HEADS, DEC_Q, wb_len, 1, -1, wb_len, dmax=WINDOW - 1))
    nb_nsa = rows_hq(_bias_strip(tbl, 0, NSA_HEADS, DEC_Q, NEW_PAD, 1, -1, 0, cmax=ds))
    nb_diff1 = _bias_strip(tbl, NSA_HEADS, DIFF_HEADS, DEC_Q, NEW_PAD, 1, -1, 0, cmax=ds)
    nb_diff = rows_hq(jnp.concatenate([nb_diff1, nb_diff1], axis=0))
    nb_mla1 = _bias_strip(tbl0, 0, 1, DEC_Q, NEW_PAD, 1, -1, 0, cmax=ds)
    nb_mla = rows_hq(jnp.tile(nb_mla1, (MLA_HEADS, 1, 1)))

    nblk_p = sp // CMP_BLOCK
    ncol_p = max(SEL_COLS_MIN, _round_up(nblk_p, LANES))
    tkb_a = min(ATT_TKB, sp)
    nbk = tkb_a // CMP_BLOCK
    assert NSA_DH + nbk <= LANES
    own = (jnp.arange(sp, dtype=I32)[:, None] % tkb_a) // CMP_BLOCK == jnp.arange(nbk, dtype=I32)[None]
    sel_kfeat = jnp.where(own, NEG, 0.0).astype(BF16)
    nblk_s = past // CMP_BLOCK
    ncol_s = max(SEL_COLS_MIN, _round_up(nblk_s, LANES))
    exp_s = (jnp.arange(ncol_s, dtype=I32)[:, None] == (jnp.arange(past, dtype=I32) // CMP_BLOCK)[None]).astype(BF16)

    xp = x_prompt.reshape(sp, d)
    xs = jnp.pad(x_sample, ((0, 0), (0, DEC_Q - ds), (0, 0))).reshape(db * DEC_Q, d)
    pos_p = jnp.arange(sp, dtype=I32)
    pos_s = jnp.tile(past + jnp.arange(DEC_Q, dtype=I32), db)
    pt = page_table.astype(I32)
    pt_prompt = jnp.arange(sp // page, dtype=I32)[None]
    pt_win = (jnp.arange(db, dtype=I32)[:, None] * (wb_len // page) + jnp.arange(wb_len // page, dtype=I32)[None])
    win_pool = cache_nsa_win.reshape(depth, db * (wb_len // page), page, cache_nsa_win.shape[3])
    mla_pool_t = jnp.swapaxes(cache_mla, 2, 3)

    outs = {k: [] for k in ("p_cmp", "p_sel", "p_win", "p_mla", "p_diff", "p_mem",
                            "s_cmp", "s_sel", "s_win", "s_mla", "s_diff")}
    for l in range(depth):
        lw = _layer_weights(l, prm)
        lam_init = 0.8 - 0.6 * math.exp(-0.3 * l)
        lmb = diff_lambda[l].astype(F32)
        lam = (jnp.exp(jnp.sum(lmb[0] * lmb[1])) - jnp.exp(jnp.sum(lmb[2] * lmb[3])) + lam_init).reshape(1)

        fp = _features(xp, pos_p, lw)
        fs = _features(xs, pos_s, lw)
        mg_p = _mm(xp, lw["w_mg"], g=lw["g_mix"], epi="sigmoid", tn=768)
        mg_s = _mm(xs, lw["w_mg"], g=lw["g_mix"], epi="sigmoid", tn=768)

        nq_h = _heads_major(fp["nq"], NSA_HEADS)[None]
        q_nsa = (nq_h * NSA_DH ** -0.5).astype(BF16)
        q_nsa2 = (nq_h * (NSA_DH ** -0.5 * LOG2E)).astype(BF16)
        bm = _block_means(fp["cmp"].reshape(1, sp // page, page, 2 * NSA_DH), 0, pt_prompt, lw["g_cmp"])
        bm = jnp.pad(bm, ((0, 0), (0, ncol_p - nblk_p), (0, 0))).astype(BF16)
        o_cmp, chosen = _cmp_attention(tbl, q_nsa, bm[:, :, :NSA_DH], bm[:, :, NSA_DH:], tq=min(CMP_TQ, sp),
                                       qbase=0, n_cmp=nblk_p, add_cur=True)
        sel16 = fp["sel"].astype(BF16)
        not_sel = (1.0 - chosen[0, :, :nblk_p]).reshape(sp, sp // tkb_a, nbk).transpose(1, 0, 2)
        q_add = jnp.pad(not_sel, ((0, 0), (0, 0), (NSA_DH, LANES - NSA_DH - nbk))).astype(BF16)
        k_sel = _pad_cols(jnp.concatenate([sel16[:, :NSA_DH], sel_kfeat], axis=-1), LANES)[None]
        o_sel = _flash(_pad_cols(q_nsa2, LANES), k_sel, sel16[None, :, NSA_DH:], s_sel, tq=ATT_TQ, tkb=ATT_TKB,
                       scale=1.0, dclamp=dclamp, qadd=q_add)
        win16 = fp["win"].astype(BF16)
        o_win = _flash(q_nsa2, win16[None, :, :NSA_DH], win16[None, :, NSA_DH:], s_win, tq=WIN_TQ, tkb=WIN_TKB,
                       scale=1.0, dclamp=dclamp_w, window=WINDOW)
        lat_p = fp["mla"][:, :MLA_KV_RANK]
        kn = _mm(lat_p, lw["w_uk"], epi="grms", gsize=MLA_NOPE, gain=jnp.tile(lw["gk_nope"], MLA_HEADS),
                 out_dtype=BF16)
        vm = _mm(lat_p, lw["w_uv"], out_dtype=BF16)
        kr16 = fp["mla"][:, MLA_KV_RANK:].astype(BF16)
        k_mla = jnp.concatenate([_heads_major(kn, MLA_HEADS),
                                 jnp.broadcast_to(kr16[None], (MLA_HEADS, sp, MLA_ROPE))], axis=-1)
        k_mla = _pad_cols(k_mla, LANES)
        q_mla = _mla_q_heads(fp["qn"], fp["qr"]).astype(BF16)[:, None]
        o_mla = _flash(q_mla, k_mla, _heads_major(vm, MLA_HEADS), s_mla, tq=MLA_TQ, tkb=MLA_TKB,
                       scale=(MLA_NOPE + MLA_ROPE) ** -0.5 * LOG2E, dclamp=LANES)
        dq = (fp["dq"].reshape(sp, DIFF_HEADS, 2, DIFF_DH).transpose(2, 1, 0, 3)
              * (DIFF_DH ** -0.5 * LOG2E)).astype(BF16)
        dk = fp["diff"][:, :2 * DIFF_DH].reshape(sp, 2, DIFF_DH).transpose(1, 0, 2).astype(BF16)
        dv = fp["diff"][None, :, 2 * DIFF_DH:].astype(BF16)
        o_diff = _flash(dq, dk, dv, s_diff, tq=ATT_TQ, tkb=ATT_TKB, scale=1.0, dclamp=dclamp)
        xp = _merge(xp, o_cmp[0], o_sel[0], o_win[0], fp["gate"], o_mla[:, 0], o_diff, lam, lw["subln"], mg_p,
                    lw["wb"], lw["wo"], 1.0 - lam_init)

        def new_rows(a):
            return jnp.pad(a.reshape(db, DEC_Q, a.shape[1])[:, :ds], ((0, 0), (0, NEW_PAD - ds), (0, 0)))

        nq_s = fs["nq"].reshape(db, DEC_Q, NSA_HEADS, NSA_DH).transpose(0, 2, 1, 3) * NSA_DH ** -0.5
        q_s4 = nq_s.astype(BF16)
        q_s = _pad_cols(nq_s.reshape(db, NSA_HEADS * DEC_Q, NSA_DH), LANES).astype(BF16)
        bm_s = _block_means(cache_nsa_cmp, l, pt, lw["g_cmp"])
        bm_s = jnp.pad(bm_s, ((0, 0), (0, ncol_s - nblk_s), (0, 0))).astype(BF16)
        o_cmp_s, chosen_s = _cmp_attention(tbl, q_s4, bm_s[:, :, :NSA_DH], bm_s[:, :, NSA_DH:], tq=DEC_Q,
                                           qbase=past, n_cmp=nblk_s, add_cur=False)
        o_sel_s = _decode_attention(cache_nsa_sel, l, pt, q_s, b_sel, new_rows(fs["sel"]), nb_nsa,
                                    kw=LANES, vlo=0, vhi=LANES, nh=NSA_HEADS, sel=chosen_s, expand=exp_s)
        o_win_s = _decode_attention(win_pool, l, pt_win, q_s, b_win, new_rows(fs["win"]), nb_nsa,
                                    kw=LANES, vlo=0, vhi=LANES, nh=NSA_HEADS)
        tok_major = lambda a: a.reshape(db, NSA_HEADS, DEC_Q, NSA_DH).transpose(1, 0, 2, 3).reshape(
            NSA_HEADS, db * DEC_Q, NSA_DH)
        o_sel_s = tok_major(o_sel_s[:, :, NSA_DH:])
        o_win_s = tok_major(o_win_s[:, :, NSA_DH:])
        o_cmp_s = o_cmp_s.transpose(1, 0, 2, 3).reshape(NSA_HEADS, db * DEC_Q, NSA_DH)
        qn_s = (fs["qn"] * jnp.tile(lw["gk_nope"], MLA_HEADS)).reshape(db, DEC_Q, MLA_HEADS, MLA_NOPE)
        qn_s = qn_s.transpose(0, 2, 1, 3)
        qbd = (qn_s[:, :, :, None, :] * jnp.eye(MLA_HEADS, dtype=F32)[None, :, None, :, None]).reshape(
            db, MLA_HEADS * DEC_Q, MLA_HEADS * MLA_NOPE).astype(BF16)
        qr_s = jnp.concatenate([fs["qr"][:, :LANES].reshape(db, DEC_Q, MLA_HEADS, half),
                                fs["qr"][:, LANES:].reshape(db, DEC_Q, MLA_HEADS, half)], axis=-1)
        qr_s = qr_s.transpose(0, 2, 1, 3).reshape(db, MLA_HEADS * DEC_Q, MLA_ROPE).astype(BF16)
        o_mla_s = _mla_decode(mla_pool_t, l, pt, qbd, qr_s, lw["w_uk"].T, lw["w_uv"],
                              new_rows(fs["mla"]).transpose(0, 2, 1), nb_mla)
        o_mla_s = o_mla_s.reshape(db, MLA_HEADS, DEC_Q, MLA_V).transpose(1, 0, 2, 3).reshape(
            MLA_HEADS, db * DEC_Q, MLA_V)
        dq_s = fs["dq"].reshape(db, DEC_Q, DIFF_HEADS, 2, DIFF_DH).transpose(0, 3, 2, 1, 4) * DIFF_DH ** -0.5
        dq_s = dq_s.reshape(db, 2, DIFF_HEADS * DEC_Q, DIFF_DH)
        zero = jnp.zeros_like(dq_s[:, 0])
        q_d = jnp.concatenate([jnp.concatenate([dq_s[:, 0], zero], axis=-1),
                               jnp.concatenate([zero, dq_s[:, 1]], axis=-1)], axis=1).astype(BF16)
        o_diff_s = _decode_attention(cache_diff, l, pt, q_d, b_diff, new_rows(fs["diff"]), nb_diff,
                                     kw=2 * DIFF_DH, vlo=2 * DIFF_DH, vhi=4 * DIFF_DH, nh=2 * DIFF_HEADS)
        o_diff_s = o_diff_s.reshape(db, 2, DIFF_HEADS, DEC_Q, 2 * DIFF_DH).transpose(1, 2, 0, 3, 4).reshape(
            2, DIFF_HEADS, db * DEC_Q, 2 * DIFF_DH)
        xs = _merge(xs, o_cmp_s, o_sel_s, o_win_s, fs["gate"], o_mla_s, o_diff_s, lam, lw["subln"], mg_s,
                    lw["wb"], lw["wo"], 1.0 - lam_init)

        g_mem_k = jnp.tile(mem_norm_qk[l, 1], MEM_HEADS)
        hw = MEM_HEADS * MEM_DH
        k_mem = _mm(mem_prompt[0], w_mem_kv[l][:, :hw], g=norm_mem_m[l], epi="grms", gsize=MEM_DH, gain=g_mem_k)
        v_mem = _mm(mem_prompt[0], w_mem_kv[l][:, hw:], g=norm_mem_m[l])
        kv_p = jnp.concatenate([k_mem, v_mem], axis=-1)
        mem_heads = lambda a: a.reshape(a.shape[0], n_mem, MEM_HEADS, MEM_DH).transpose(0, 2, 1, 3).astype(BF16)
        wq4 = w_mem_q[l].reshape(d, MEM_HEADS, MEM_DH).transpose(1, 0, 2).astype(BF16)
        wo4 = w_mem_o[l].reshape(MEM_HEADS, MEM_DH, d).astype(BF16)
        g_x = norm_mem_x[l].reshape(1, d).astype(F32)
        g_q = mem_norm_qk[l, 0].reshape(1, MEM_DH).astype(F32)
        tmx = min(256, sp)
        xp = _cross_attend(xp.reshape(sp // tmx, tmx, d), g_x, wq4, g_q, mem_heads(k_mem[None]),
                           mem_heads(v_mem[None]), wo4).reshape(sp, d)
        cm = cache_mem[l]
        xs = _cross_attend(xs.reshape(db, DEC_Q, d), g_x, wq4, g_q, mem_heads(cm[:, :, :hw]),
                           mem_heads(cm[:, :, hw:]), wo4).reshape(db * DEC_Q, d)

        g_f = norm_ffn[l].reshape(1, d).astype(F32)
        j = l // 2
        if l % 2 == 0:
            wg, wu, wd = w_ffn_gate[j].astype(BF16), w_ffn_up[j].astype(BF16), w_ffn_down[j].astype(BF16)
            xp = _dense_ffn(xp, g_f, wg, wu, wd)
            xs = _dense_ffn(xs, g_f, wg, wu, wd)
        else:
            wg, wu, wd = w_exp_gate[j].astype(BF16), w_exp_up[j].astype(BF16), w_exp_down[j].astype(BF16)
            xp = _moe_ffn(xp, g_f, w_router[j], wg, wu, wd)
            xs = _moe_ffn(xs, g_f, w_router[j], wg, wu, wd)

        outs["p_cmp"].append(fp["cmp"][None])
        outs["p_sel"].append(fp["sel"][None])
        outs["p_win"].append(fp["win"][None, -min(WINDOW, sp):])
        outs["p_mla"].append(fp["mla"][None])
        outs["p_diff"].append(fp["diff"][None])
        outs["p_mem"].append(kv_p[None])
        srows = lambda a: a.reshape(db, DEC_Q, a.shape[1])[:, :ds]
        outs["s_cmp"].append(srows(fs["cmp"]))
        outs["s_sel"].append(srows(fs["sel"]))
        outs["s_win"].append(srows(fs["win"]))
        outs["s_mla"].append(srows(fs["mla"]))
        outs["s_diff"].append(srows(fs["diff"]))

    y_p = xp.reshape(bp, sp, d)
    y_s = xs.reshape(db, DEC_Q, d)[:, :ds]
    st = lambda k: jnp.stack(outs[k])
    return (y_p, y_s, st("p_cmp"), st("p_sel"), st("p_win"), st("p_mla"), st("p_diff"), st("p_mem"),
            st("s_cmp"), st("s_sel"), st("s_win"), st("s_mla"), st("s_diff"))
```

```python
import functools
import math

import numpy as np
import jax
import jax.numpy as jnp
from jax import lax
from jax.experimental import pallas as pl
from jax.experimental.pallas import tpu as pltpu

F32 = jnp.float32
BF16 = jnp.bfloat16
I32 = jnp.int32

NSA_HEADS = 8
NSA_DH = 64
CMP_BLOCK = 64
N_SEL = 16
WINDOW = 512
MLA_HEADS = 8
MLA_NOPE = 64
MLA_ROPE = 32
MLA_V = 64
MLA_KV_RANK = 128
MLA_Q_RANK = 256
ROPE_BASE = 10000.0
DIFF_HEADS = 4
DIFF_DH = 64
MEM_HEADS = 4
MEM_DH = 64
N_BUCKETS = 32
MAX_DISTANCE = 2048
N_EXPERTS = 8
EPS = 1e-6
NEG = -1e30
LOG2E = 1.4426950408889634

LANES = 128
SUBLANES = 8
VMEM_LIMIT = 56 * 1024 * 1024

ATT_TQ, ATT_TKB = 128, 1024
WIN_TQ, WIN_TKB = 256, 256
MLA_TQ, MLA_TKB = 512, 1024
FLASH_CHUNK = 256
CMP_TQ = 256
DEC_PAGES = 32
MLA_DEC_PAGES = 32
DEC_SPLIT = 2
CMP_DEC_BATCH = 8
DEC_Q = 8
NEW_PAD = 128
SEL_COLS_MIN = 128

NT_DIMS = (((1,), (1,)), ((), ()))


def _cparams(sem):
    return pltpu.CompilerParams(dimension_semantics=sem, vmem_limit_bytes=VMEM_LIMIT)


def _round_up(x, m):
    return (x + m - 1) // m * m


def _bucket_thresholds():
    n = np.arange(0, 2 * MAX_DISTANCE, dtype=np.int64)
    exact = N_BUCKETS // 2
    nf = np.maximum(n, 1).astype(np.float32)
    large = exact + (np.log(nf / np.float32(exact)) / np.float32(math.log(MAX_DISTANCE / exact))
                     * np.float32(N_BUCKETS - exact)).astype(np.int32)
    b = np.where(n < exact, n, np.minimum(large, N_BUCKETS - 1))
    assert np.all(np.diff(b) >= 0)
    return tuple(int(np.argmax(b >= j)) for j in range(1, N_BUCKETS))


BUCKET_THR = _bucket_thresholds()
BIAS_FLAT_DIST = BUCKET_THR[-1]


def _bias_from_dist(d, tbl_ref, col):
    v = jnp.full(d.shape, tbl_ref[0, col], F32)
    for j in range(1, N_BUCKETS):
        v = jnp.where(d >= BUCKET_THR[j - 1], tbl_ref[j, col], v)
    return v


def _split_hi_lo(a):
    hi = a.astype(BF16)
    lo = (a - hi.astype(F32)).astype(BF16)
    return hi, lo


def _dot_hl(a, b01):
    hi, lo = _split_hi_lo(a)
    return (jnp.dot(hi, b01, preferred_element_type=F32)
            + jnp.dot(lo, b01, preferred_element_type=F32))


def _rms(x, g):
    return x * lax.rsqrt(jnp.mean(x * x, axis=-1, keepdims=True) + EPS) * g


def _group_rms(y, ind, ind_t, gain, gsize):
    ssq = _dot_hl(y * y, ind)
    rs = lax.rsqrt(ssq * (1.0 / gsize) + EPS)
    return y * _dot_hl(rs, ind_t) * gain


def _group_indicator(n, gsize):
    ind = np.zeros((n, LANES), np.float32)
    ind[np.arange(n), np.arange(n) // gsize] = 1.0
    return jnp.asarray(ind, BF16), jnp.asarray(ind.T.copy(), BF16)


def _lane_tile(x, n):
    return x if n == 1 else jnp.concatenate([x] * n, axis=1)


def _softmax_step(s, m_ref, l_ref):
    m_prev = m_ref[...]
    m_new = jnp.maximum(m_prev, jnp.max(s, axis=-1, keepdims=True))
    alpha = jnp.exp(m_prev - m_new)
    p = jnp.exp(s - _lane_tile(m_new, s.shape[1] // LANES))
    l_ref[...] = alpha * l_ref[...] + jnp.sum(p, axis=-1, keepdims=True)
    m_ref[...] = m_new
    return p, alpha


def _init_softmax_state(m_ref, l_ref, acc_ref):
    m_ref[...] = jnp.full(m_ref.shape, NEG, F32)
    l_ref[...] = jnp.zeros(l_ref.shape, F32)
    acc_ref[...] = jnp.zeros(acc_ref.shape, F32)


def _merge_softmax_states(m_ref, l_ref, acc_ref):
    ns = m_ref.shape[0]
    dv = acc_ref.shape[-1]
    m_all = m_ref[0]
    for h in range(1, ns):
        m_all = jnp.maximum(m_all, m_ref[h])
    num = den = None
    for h in range(ns):
        w = jnp.exp(m_ref[h] - m_all)
        n_h, d_h = acc_ref[h] * w[:, :dv], l_ref[h] * w
        num, den = (n_h, d_h) if num is None else (num + n_h, den + d_h)
    return num / den[:, :dv]


def _strip_body(tbl_ref, o_ref, *, hoff, ar, ac, c0, dmax, cmax, rb, mult):
    h = pl.program_id(0)
    cols = o_ref.shape[2]
    r = lax.broadcasted_iota(I32, (rb, cols), 0) + pl.program_id(1) * rb
    c = lax.broadcasted_iota(I32, (rb, cols), 1)
    d = ar * r + ac * c + c0
    v = _bias_from_dist(d, tbl_ref, hoff + h)
    if mult != 1.0:
        v = v * mult
    ok = jnp.where(d >= 0, jnp.where(d <= dmax, jnp.where(c < cmax, 1, 0), 0), 0)
    o_ref[0] = jnp.where(ok > 0, v, NEG)


def _bias_strip(tbl, hoff, nh, rows, cols, ar, ac, c0, dmax=2 ** 30, cmax=2 ** 30, mult=1.0):
    rb = rows if rows <= 256 else 256
    assert rows % rb == 0
    return pl.pallas_call(
        functools.partial(_strip_body, hoff=hoff, ar=ar, ac=ac, c0=c0, dmax=dmax, cmax=cmax, rb=rb, mult=mult),
        grid=(nh, rows // rb),
        in_specs=[pl.BlockSpec(memory_space=pltpu.SMEM)],
        out_specs=pl.BlockSpec((1, rb, cols), lambda h, r: (h, r, 0)),
        out_shape=jax.ShapeDtypeStruct((nh, rows, cols), F32),
        compiler_params=_cparams(("arbitrary", "arbitrary")),
        name="bias_strip",
    )(tbl)


def _mm_body(*refs, norm, epi, gsize):
    refs = list(refs)
    x_ref = refs.pop(0)
    g_ref = refs.pop(0) if norm else None
    w_ref = refs.pop(0)
    if epi == "grms":
        ind_ref, indt_ref, gain_ref = refs.pop(0), refs.pop(0), refs.pop(0)
    o_ref, xn_ref = refs

    @pl.when(pl.program_id(1) == 0)
    def _():
        x = x_ref[...]
        if norm:
            x = _rms(x, g_ref[...])
        xn_ref[...] = x.astype(BF16)

    y = jnp.dot(xn_ref[...], w_ref[...], preferred_element_type=F32)
    if epi == "sigmoid":
        y = jax.nn.sigmoid(y)
    elif epi == "grms":
        y = _group_rms(y, ind_ref[...], indt_ref[...], gain_ref[...], gsize)
    o_ref[...] = y.astype(o_ref.dtype)


def _mm(x, w, *, g=None, epi=None, gsize=1, gain=None, out_dtype=F32, tm=512, tn=None):
    m, k = x.shape
    n = w.shape[1]
    tm = min(tm, m)
    tn = n if (tn is None or epi == "grms") else tn
    assert m % tm == 0 and n % tn == 0
    args = [x]
    specs = [pl.BlockSpec((tm, k), lambda i, j: (i, 0))]
    if g is not None:
        args.append(g.reshape(1, k).astype(F32))
        specs.append(pl.BlockSpec((1, k), lambda i, j: (0, 0)))
    args.append(w.astype(BF16))
    specs.append(pl.BlockSpec((k, tn), lambda i, j: (0, j)))
    if epi == "grms":
        ind, ind_t = _group_indicator(n, gsize)
        args += [ind, ind_t, gain.reshape(1, n).astype(F32)]
        specs += [pl.BlockSpec((n, LANES), lambda i, j: (0, 0)),
                  pl.BlockSpec((LANES, n), lambda i, j: (0, 0)),
                  pl.BlockSpec((1, n), lambda i, j: (0, 0))]
    return pl.pallas_call(
        functools.partial(_mm_body, norm=g is not None, epi=epi, gsize=gsize),
        grid=(m // tm, n // tn),
        in_specs=specs,
        out_specs=pl.BlockSpec((tm, tn), lambda i, j: (i, j)),
        out_shape=jax.ShapeDtypeStruct((m, n), out_dtype),
        scratch_shapes=[pltpu.VMEM((tm, k), BF16)],
        compiler_params=_cparams(("arbitrary", "arbitrary")),
        name="norm_matmul",
    )(*args)


FA_NQ, FA_CMP, FA_SEL, FA_WIN, FA_CQ, FA_CKV, FA_DQ, FA_DK, FA_DV, FA_NG, FA_KR, FA_END = (
    0, 512, 640, 768, 896, 1152, 1280, 1792, 1920, 2048, 2176, 2304)
GN_NQ, GN_KS, GN_KW, GN_CQ, GN_CKV, GN_QN, GN_QR, GN_KR, GN_DQ, GN_DK, GN_END = (
    0, 512, 640, 768, 1024, 1152, 1664, 1920, 2048, 2560, 2688)


def _half_rms(x, g, lo_half):
    lane = lax.broadcasted_iota(I32, x.shape, 1)
    sel = (lane < NSA_DH) if lo_half else (lane >= NSA_DH)
    ssq = jnp.sum(jnp.where(sel, x * x, 0.0), axis=-1, keepdims=True)
    return jnp.where(sel, x * lax.rsqrt(ssq * (1.0 / NSA_DH) + EPS) * g, x)


def _feat_body(x_ref, gmix_ref, wa_ref, wuq_ref, cos_ref, sin_ref, gains_ref,
               i512_ref, i512t_ref, i16_ref, i16t_ref,
               cmp_ref, sel_ref, win_ref, mla_ref, diff_ref,
               nq_ref, gate_ref, qn_ref, qr_ref, dq_ref):
    gains = gains_ref[...]
    gslice = lambda a, b: gains[:, a:b]
    xn = _rms(x_ref[...], gmix_ref[...]).astype(BF16)
    h = jnp.dot(xn, wa_ref[...], preferred_element_type=F32)
    i512, i512t = i512_ref[...], i512t_ref[...]
    i16, i16t = i16_ref[...], i16t_ref[...]
    cos, sin = cos_ref[...], sin_ref[...]
    half = MLA_ROPE // 2

    nq_ref[...] = _group_rms(h[:, FA_NQ:FA_CMP], i512, i512t, gslice(GN_NQ, GN_KS), NSA_DH)
    cmp_ref[...] = h[:, FA_CMP:FA_SEL]
    sel_ref[...] = _half_rms(h[:, FA_SEL:FA_WIN], gslice(GN_KS, GN_KW), True)
    win_ref[...] = _half_rms(h[:, FA_WIN:FA_CQ], gslice(GN_KW, GN_CQ), True)
    gate_ref[...] = jax.nn.sigmoid(h[:, FA_NG:FA_KR])

    cqn = _rms(h[:, FA_CQ:FA_CKV], gslice(GN_CQ, GN_CKV)).astype(BF16)
    q = jnp.dot(cqn, wuq_ref[...], preferred_element_type=F32)
    nope_w = MLA_HEADS * MLA_NOPE
    qn_ref[...] = _group_rms(q[:, :nope_w], i512, i512t, gslice(GN_QN, GN_QR), MLA_NOPE)
    x1 = q[:, nope_w:nope_w + LANES]
    x2 = q[:, nope_w + LANES:nope_w + 2 * LANES]
    rs = lax.rsqrt(_dot_hl(x1 * x1 + x2 * x2, i16) * (1.0 / MLA_ROPE) + EPS)
    sc = _dot_hl(rs, i16t)
    x1 = x1 * sc * gslice(GN_QR, GN_QR + LANES)
    x2 = x2 * sc * gslice(GN_QR + LANES, GN_KR)
    qr_ref[:, 0:LANES] = x1 * cos - x2 * sin
    qr_ref[:, LANES:2 * LANES] = x1 * sin + x2 * cos
    mla_ref[:, 0:MLA_KV_RANK] = _rms(h[:, FA_CKV:FA_DQ], gslice(GN_CKV, GN_QN))
    kr = h[:, FA_KR:FA_END]
    krn = kr * lax.rsqrt(jnp.sum(kr * kr, axis=-1, keepdims=True) * (1.0 / MLA_ROPE) + EPS) * gslice(GN_KR, GN_DQ)
    k1, k2 = krn[:, 0:half], krn[:, half:MLA_ROPE]
    c16, s16 = cos[:, 0:half], sin[:, 0:half]
    mla_ref[:, MLA_KV_RANK:MLA_KV_RANK + MLA_ROPE] = jnp.concatenate(
        [k1 * c16 - k2 * s16, k1 * s16 + k2 * c16], axis=-1)

    dq_ref[...] = _group_rms(h[:, FA_DQ:FA_DK], i512, i512t, gslice(GN_DQ, GN_DK), DIFF_DH)
    dk = h[:, FA_DK:FA_DV]
    gdk = gslice(GN_DK, GN_END)
    dkn = _half_rms(_half_rms(dk, gdk, True), gdk, False)
    diff_ref[:, 0:2 * DIFF_DH] = dkn
    diff_ref[:, 2 * DIFF_DH:4 * DIFF_DH] = h[:, FA_DV:FA_NG]


def _features(x, pos, lw, tm=256):
    t, d = x.shape
    tm = min(tm, t)
    assert t % tm == 0
    half = MLA_ROPE // 2
    inv = ROPE_BASE ** (-jnp.arange(half, dtype=F32) / half)
    ang = pos.astype(F32)[:, None] * inv
    cos = jnp.tile(jnp.cos(ang), (1, LANES // half))
    sin = jnp.tile(jnp.sin(ang), (1, LANES // half))
    i512, i512t = _group_indicator(512, 64)
    i16, i16t = _group_indicator(LANES, half)
    row = lambda w: pl.BlockSpec((tm, w), lambda i: (i, 0))
    full = lambda a: pl.BlockSpec(a.shape, lambda i: (0,) * a.ndim)
    consts = [lw["g_mix"], lw["w_a"], lw["w_uq"]]
    tail = [lw["gains"], i512, i512t, i16, i16t]
    outs = [(128, "cmp"), (128, "sel"), (128, "win"), (MLA_KV_RANK + MLA_ROPE, "mla"), (256, "diff"),
            (512, "nq"), (128, "gate"), (512, "qn"), (256, "qr"), (512, "dq")]
    res = pl.pallas_call(
        _feat_body,
        grid=(t // tm,),
        in_specs=[row(d)] + [full(a) for a in consts] + [row(LANES), row(LANES)] + [full(a) for a in tail],
        out_specs=[row(w) for w, _ in outs],
        out_shape=[jax.ShapeDtypeStruct((t, w), F32) for w, _ in outs],
        compiler_params=_cparams(("arbitrary",)),
        name="mixer_features",
    )(x, *consts, cos, sin, *tail)
    return {name: r for (_, name), r in zip(outs, res)}


def _page_copy(pool_ref, layer, pid, buf_ref, slot, p, sem_ref):
    return pltpu.make_async_copy(pool_ref.at[layer, pid], buf_ref.at[slot, p], sem_ref.at[slot])


def _gather_pages(pt_ref, pool_ref, buf_ref, sem_ref, *, layer, npg):
    b, c = pl.program_id(0), pl.program_id(1)
    nb, nc = pl.num_programs(0), pl.num_programs(1)
    t = b * nc + c
    slot = t % 2

    def start(bb, cc, sl):
        for p in range(npg):
            _page_copy(pool_ref, layer, pt_ref[bb, cc * npg + p], buf_ref, sl, p, sem_ref).start()

    @pl.when(t == 0)
    def _():
        start(b, c, slot)

    @pl.when(t + 1 < nb * nc)
    def _():
        wrap = c == nc - 1
        start(jnp.where(wrap, b + 1, b), jnp.where(wrap, 0, c + 1), 1 - slot)

    for p in range(npg):
        _page_copy(pool_ref, layer, 0, buf_ref, slot, p, sem_ref).wait()
    return slot


def _page_scratch(npg, rows, cols):
    return [pltpu.VMEM((2, npg, rows, cols), F32), pltpu.SemaphoreType.DMA((2,))]


def _blockmean_body(pt_ref, pool_ref, g_ref, o_ref, buf_ref, sem_ref, *, layer, npg, page):
    slot = _gather_pages(pt_ref, pool_ref, buf_ref, sem_ref, layer=layer, npg=npg)
    per = page // CMP_BLOCK
    for p in range(npg):
        pg = buf_ref[slot, p]
        o_ref[0, p * per:(p + 1) * per, :] = jnp.mean(pg.reshape(per, CMP_BLOCK, pg.shape[-1]), axis=1)
    o_ref[0] = _half_rms(o_ref[0], g_ref[...], True)


def _block_means(pool, layer, page_table, g_cmp):
    nb, n_pages = page_table.shape
    page, w = pool.shape[2], pool.shape[3]
    npg = min(DEC_PAGES, n_pages)
    assert n_pages % npg == 0
    per = page // CMP_BLOCK
    gain = jnp.concatenate([g_cmp.astype(F32), jnp.ones((NSA_DH,), F32)]).reshape(1, 2 * NSA_DH)
    return pl.pallas_call(
        functools.partial(_blockmean_body, layer=layer, npg=npg, page=page),
        grid_spec=pltpu.PrefetchScalarGridSpec(
            num_scalar_prefetch=1,
            grid=(nb, n_pages // npg),
            in_specs=[pl.BlockSpec(memory_space=pl.ANY), pl.BlockSpec((1, w), lambda b, c, pt: (0, 0))],
            out_specs=pl.BlockSpec((1, npg * per, w), lambda b, c, pt: (b, c, 0)),
            scratch_shapes=_page_scratch(npg, page, w),
        ),
        out_shape=jax.ShapeDtypeStruct((nb, n_pages * per, w), F32),
        compiler_params=_cparams(("arbitrary", "arbitrary")),
        name="block_means",
    )(page_table, pool, gain)


def _cmp_body(tbl_ref, q_ref, kc_ref, vc_ref, o_ref, sel_ref, *, nh, tq, qbase, n_cmp, add_cur):
    ncol = kc_ref.shape[1]
    qpos = qbase + pl.program_id(1) * tq + lax.broadcasted_iota(I32, (tq, ncol), 0)
    col = lax.broadcasted_iota(I32, (tq, ncol), 1)
    cur = qpos // CMP_BLOCK
    vis = jnp.where(col < cur, jnp.where(col < n_cmp, 1.0, 0.0), 0.0)
    dist = qpos - (col * CMP_BLOCK + CMP_BLOCK - 1)
    bias = [_bias_from_dist(dist, tbl_ref, h) for h in range(nh)]
    for b in range(q_ref.shape[0]):
        psum = jnp.zeros((tq, ncol), F32)
        for h in range(nh):
            s = lax.dot_general(q_ref[b, h], kc_ref[b], NT_DIMS, preferred_element_type=F32)
            s = jnp.where(vis > 0, s + bias[h], NEG)
            p = jnp.exp(s - jnp.max(s, axis=-1, keepdims=True)) * vis
            p = p / jnp.maximum(jnp.sum(p, axis=-1, keepdims=True), 1e-30)
            o_ref[b, h] = jnp.dot(p.astype(BF16), vc_ref[b], preferred_element_type=F32)
            psum = psum + p

        score = jnp.where(vis > 0, psum, -1.0)
        chosen = jnp.zeros((tq, ncol), F32)
        for _ in range(min(N_SEL - 1, n_cmp)):
            m = jnp.max(score, axis=-1, keepdims=True)
            first = jnp.min(jnp.where(score == m, col, ncol), axis=-1, keepdims=True)
            hit = col == first
            chosen = jnp.where(hit, jnp.where(m >= 0.0, 1.0, 0.0), chosen)
            score = jnp.where(hit, -2.0, score)
        if add_cur:
            chosen = jnp.where(col == cur, 1.0, chosen)
        sel_ref[b] = chosen.astype(BF16)


def _cmp_attention(tbl, q, kc, vc, *, tq, qbase, n_cmp, add_cur, sb=1):
    nb, nh, t, _ = q.shape
    ncol = kc.shape[1]
    assert nb % sb == 0
    return pl.pallas_call(
        functools.partial(_cmp_body, nh=nh, tq=tq, qbase=qbase, n_cmp=n_cmp, add_cur=add_cur),
        grid=(nb // sb, t // tq),
        in_specs=[pl.BlockSpec(memory_space=pltpu.SMEM),
                  pl.BlockSpec((sb, nh, tq, NSA_DH), lambda b, i: (b, 0, i, 0)),
                  pl.BlockSpec((sb, ncol, NSA_DH), lambda b, i: (b, 0, 0)),
                  pl.BlockSpec((sb, ncol, NSA_DH), lambda b, i: (b, 0, 0))],
        out_specs=[pl.BlockSpec((sb, nh, tq, NSA_DH), lambda b, i: (b, 0, i, 0)),
                   pl.BlockSpec((sb, tq, ncol), lambda b, i: (b, i, 0))],
        out_shape=[jax.ShapeDtypeStruct((nb, nh, t, NSA_DH), F32),
                   jax.ShapeDtypeStruct((nb, t, ncol), BF16)],
        compiler_params=_cparams(("arbitrary", "arbitrary")),
        name="cmp_attention",
    )(tbl, q, kc, vc)


def _flash_body(qi_ref, kj_ref, fl_ref, *refs, ng, nh, gv, tq, tkb, scale, dclamp, has_qadd):
    refs = list(refs)
    q_ref, k_ref, v_ref, strip_ref = refs[:4]
    refs = refs[4:]
    if has_qadd:
        qadd_ref = refs.pop(0)
    o_ref, m_ref, l_ref, acc_ref = refs
    step = pl.program_id(0)
    flags = fl_ref[step]
    delta0 = qi_ref[step] * tq - kj_ref[step] * tkb
    ch = min(FLASH_CHUNK, tkb)
    dv = o_ref.shape[-1]
    dvp = acc_ref.shape[-1]
    ones_col = dvp > dv

    @pl.when(flags % 2 == 1)
    def _():
        _init_softmax_state(m_ref, l_ref, acc_ref)

    for g in range(ng):
        q = q_ref[g]
        if has_qadd:
            q = q + qadd_ref[0][None]
        q = q.reshape(nh * tq, q_ref.shape[-1])
        chunks = []
        m_cur = None
        for c in range(tkb // ch):
            s = lax.dot_general(q, k_ref[g, c * ch:(c + 1) * ch, :], NT_DIMS, preferred_element_type=F32)
            if scale != 1.0:
                s = s * scale
            slabs = []
            for u in range(ch // LANES):
                delta = delta0 - (c * ch + u * LANES)
                r0 = pl.multiple_of(jnp.clip(delta, -tq, dclamp) + tq, SUBLANES)
                slabs.append(strip_ref[:, pl.ds(r0, tq), :])
            bias = slabs[0] if len(slabs) == 1 else jnp.concatenate(slabs, axis=-1)
            s = (s.reshape(nh, tq, ch) + bias).reshape(nh * tq, ch)
            chunks.append(s)
            mc = jnp.max(s, axis=-1, keepdims=True)
            m_cur = mc if m_cur is None else jnp.maximum(m_cur, mc)
        m_prev = m_ref[g]
        m_new = jnp.maximum(m_prev, m_cur)
        alpha = jnp.exp2(m_prev - m_new)
        m_b = _lane_tile(m_new, ch // LANES)
        ps = []
        l_add = None
        for s in chunks:
            p = jnp.exp2(s - m_b)
            if not ones_col:
                ls = jnp.sum(p, axis=-1, keepdims=True)
                l_add = ls if l_add is None else l_add + ls
            ps.append(p.astype(BF16))
        p_all = ps[0] if len(ps) == 1 else jnp.concatenate(ps, axis=-1)
        pv = jnp.dot(p_all, v_ref[g if gv > 1 else 0], preferred_element_type=F32)
        if not ones_col:
            l_ref[g] = l_ref[g] * alpha + l_add
        acc_ref[g] = acc_ref[g] * _lane_tile(alpha, dvp // LANES) + pv
        m_ref[g] = m_new

    @pl.when(flags >= 2)
    def _():
        for g in range(ng):
            acc = acc_ref[g]
            den = acc[:, dv:dv + 1] if ones_col else l_ref[g][:, 0:1]
            o_ref[g] = (acc[:, :dv] / den).reshape(nh, tq, dv)


def _flash_pairs(t, tq, tkb, window):
    qi, kj, fl = [], [], []
    for i in range(t // tq):
        q0 = i * tq
        jmax = (q0 + tq - 1) // tkb
        jmin = 0 if window is None else max(0, (q0 - (window - 1)) // tkb)
        for j in range(jmin, jmax + 1):
            qi.append(i)
            kj.append(j)
            fl.append((1 if j == jmin else 0) + (2 if j == jmax else 0))
    return (jnp.asarray(np.array(a, np.int32)) for a in (qi, kj, fl))


def _with_ones_column(v):
    pad = jnp.zeros(v.shape[:-1] + (LANES,), v.dtype).at[..., 0].set(1)
    return jnp.concatenate([v, pad], axis=-1) if v.shape[-1] % LANES == 0 else jnp.concatenate(
        [v, pad[..., :LANES - v.shape[-1] % LANES]], axis=-1)


def _flash(q, k, v, strip, *, tq, tkb, scale, dclamp, window=None, qadd=None):
    ng, nh, t, d = q.shape
    gv, _, dv = v.shape
    tq, tkb = min(tq, t), min(tkb, t)
    assert t % tq == 0 and t % tkb == 0 and tq % LANES == 0 and tkb % min(FLASH_CHUNK, tkb) == 0
    assert strip.shape[2] == LANES and strip.shape[1] >= dclamp + 2 * tq
    v1 = _with_ones_column(v) if dv < LANES else v
    dvp = v1.shape[-1]
    qi, kj, fl = _flash_pairs(t, tq, tkb, window)
    args = [q, k, v1, strip]
    specs = [pl.BlockSpec((ng, nh, tq, d), lambda s, qi, kj, fl: (0, 0, qi[s], 0)),
             pl.BlockSpec((ng, tkb, d), lambda s, qi, kj, fl: (0, kj[s], 0)),
             pl.BlockSpec((gv, tkb, dvp), lambda s, qi, kj, fl: (0, kj[s], 0)),
             pl.BlockSpec(strip.shape, lambda s, qi, kj, fl: (0, 0, 0))]
    if qadd is not None:
        args.append(qadd)
        specs.append(pl.BlockSpec((1, tq, d), lambda s, qi, kj, fl: (kj[s], qi[s], 0)))
    return pl.pallas_call(
        functools.partial(_flash_body, ng=ng, nh=nh, gv=gv, tq=tq, tkb=tkb, scale=scale,
                          dclamp=dclamp, has_qadd=qadd is not None),
        grid_spec=pltpu.PrefetchScalarGridSpec(
            num_scalar_prefetch=3,
            grid=(qi.shape[0],),
            in_specs=specs,
            out_specs=pl.BlockSpec((ng, nh, tq, dv), lambda s, qi, kj, fl: (0, 0, qi[s], 0)),
            scratch_shapes=[pltpu.VMEM((ng, nh * tq, LANES), F32),
                            pltpu.VMEM((ng, nh * tq, LANES) if dvp == dv else (1, SUBLANES, LANES), F32),
                            pltpu.VMEM((ng, nh * tq, dvp), F32)],
        ),
        out_shape=jax.ShapeDtypeStruct((ng, nh, t, dv), F32),
        compiler_params=_cparams(("arbitrary",)),
        name="prompt_attention",
    )(qi, kj, fl, *args)


def _decode_body(pt_ref, *refs, layer, npg, page, kw, vlo, vhi, nh, has_sel):
    refs = list(refs)
    q_ref, bias_ref = refs[:2]
    refs = refs[2:]
    if has_sel:
        sel_ref, e_ref = refs[:2]
        refs = refs[2:]
    new_ref, nbias_ref, pool_ref, o_ref, cb_ref, m_ref, l_ref, acc_ref, buf_ref, sem_ref = refs
    c = pl.program_id(1)
    dv = vhi - vlo
    slot = _gather_pages(pt_ref, pool_ref, buf_ref, sem_ref, layer=layer, npg=npg)

    @pl.when(c == 0)
    def _():
        _init_softmax_state(m_ref, l_ref, acc_ref)

    for p in range(npg):
        cb_ref[p * page:(p + 1) * page, :] = buf_ref[slot, p].astype(BF16)
    q = q_ref[0]
    ns = m_ref.shape[0]
    hk = npg * page // ns
    for h in range(ns):
        lo, hi = h * hk, (h + 1) * hk
        s = lax.dot_general(q, cb_ref[lo:hi, 0:kw], NT_DIMS, preferred_element_type=F32) + bias_ref[:, lo:hi]
        if has_sel:
            picked = jnp.dot(sel_ref[0], e_ref[:, lo:hi], preferred_element_type=F32)
            nrow = s.shape[0]
            s = (s.reshape(nh, nrow // nh, hk) + ((picked - 1.0) * (-NEG))[None]).reshape(nrow, hk)
        p, alpha = _softmax_step(s, m_ref.at[h], l_ref.at[h])
        acc_ref[h] = acc_ref[h] * alpha[:, :dv] + jnp.dot(p.astype(BF16), cb_ref[lo:hi, vlo:vhi],
                                                          preferred_element_type=F32)

    @pl.when(c == pl.num_programs(1) - 1)
    def _():
        nw = new_ref[0].astype(BF16)
        s2 = lax.dot_general(q, nw[:, 0:kw], NT_DIMS, preferred_element_type=F32) + nbias_ref[...]
        p2, alpha2 = _softmax_step(s2, m_ref.at[0], l_ref.at[0])
        acc_ref[0] = acc_ref[0] * alpha2[:, :dv] + jnp.dot(p2.astype(BF16), nw[:, vlo:vhi],
                                                           preferred_element_type=F32)
        o_ref[0] = _merge_softmax_states(m_ref, l_ref, acc_ref)


def _decode_attention(pool, layer, page_table, q, bias, new_rows, new_bias, *, kw, vlo, vhi, nh,
                      sel=None, expand=None):
    nb, n_pages = page_table.shape
    page, w = pool.shape[2], pool.shape[3]
    r = q.shape[1]
    npg = min(DEC_PAGES, n_pages)
    assert n_pages % npg == 0
    ck = npg * page
    args = [q, bias]
    specs = [pl.BlockSpec((1, r, kw), lambda b, c, pt: (b, 0, 0)),
             pl.BlockSpec((r, ck), lambda b, c, pt: (0, c))]
    if sel is not None:
        args += [sel, expand]
        specs += [pl.BlockSpec((1, DEC_Q, sel.shape[2]), lambda b, c, pt: (b, 0, 0)),
                  pl.BlockSpec((expand.shape[0], ck), lambda b, c, pt: (0, c))]
    args += [new_rows, new_bias]
    specs += [pl.BlockSpec((1, NEW_PAD, w), lambda b, c, pt: (b, 0, 0)),
              pl.BlockSpec((r, NEW_PAD), lambda b, c, pt: (0, 0))]
    args.append(pool)
    specs.append(pl.BlockSpec(memory_space=pl.ANY))
    dv = vhi - vlo
    return pl.pallas_call(
        functools.partial(_decode_body, layer=layer, npg=npg, page=page, kw=kw, vlo=vlo, vhi=vhi, nh=nh,
                          has_sel=sel is not None),
        grid_spec=pltpu.PrefetchScalarGridSpec(
            num_scalar_prefetch=1,
            grid=(nb, n_pages // npg),
            in_specs=specs,
            out_specs=pl.BlockSpec((1, r, dv), lambda b, c, pt: (b, 0, 0)),
            scratch_shapes=[pltpu.VMEM((ck, w), BF16), pltpu.VMEM((DEC_SPLIT, r, LANES), F32),
                            pltpu.VMEM((DEC_SPLIT, r, LANES), F32), pltpu.VMEM((DEC_SPLIT, r, dv), F32)]
            + _page_scratch(npg, page, w),
        ),
        out_shape=jax.ShapeDtypeStruct((nb, r, dv), F32),
        compiler_params=_cparams(("arbitrary", "arbitrary")),
        name="decode_attention",
    )(page_table, *args)


def _mla_decode_body(pt_ref, qbd_ref, qr_ref, wuk_ref, wuv_ref, new_ref, nbias_ref, pool_ref,
                     o_ref, cb_ref, w_ref, m_ref, l_ref, acc_ref, buf_ref, sem_ref, *, layer, npg, page):
    c = pl.program_id(1)
    slot = _gather_pages(pt_ref, pool_ref, buf_ref, sem_ref, layer=layer, npg=npg)
    scale = (MLA_NOPE + MLA_ROPE) ** -0.5
    nrow = qbd_ref.shape[1]
    per = nrow // MLA_HEADS
    nk = MLA_HEADS * MLA_NOPE

    @pl.when(c == 0)
    def _():
        _init_softmax_state(m_ref, l_ref, acc_ref)
        w_ref[...] = jnp.zeros(w_ref.shape, BF16)
        w_ref[0:nk, 0:MLA_KV_RANK] = wuk_ref[...]
        absorbed = jnp.dot(qbd_ref[0], wuk_ref[...], preferred_element_type=F32)
        w_ref[nk:nk + nrow, 0:MLA_KV_RANK] = absorbed.astype(BF16)
        w_ref[nk + nrow:nk + 2 * nrow, MLA_KV_RANK:MLA_KV_RANK + MLA_ROPE] = qr_ref[0]

    def scores(feat_t):
        n = feat_t.shape[1]
        big = jnp.dot(w_ref[...], feat_t, preferred_element_type=F32)
        knt = big[0:nk]
        ssq = jnp.sum((knt * knt).reshape(MLA_HEADS, MLA_NOPE, n), axis=1)
        rs = lax.rsqrt(ssq * (1.0 / MLA_NOPE) + EPS)
        sn = (big[nk:nk + nrow].reshape(MLA_HEADS, per, n) * rs[:, None, :]).reshape(nrow, n)
        return (sn + big[nk + nrow:nk + 2 * nrow]) * scale

    def accumulate(s, feat_t, h):
        p, alpha = _softmax_step(s, m_ref.at[h], l_ref.at[h])
        ctx = lax.dot_general(p.astype(BF16), feat_t[0:MLA_KV_RANK], NT_DIMS, preferred_element_type=F32)
        acc_ref[h] = acc_ref[h] * alpha + ctx

    for p in range(npg):
        cb_ref[:, p * page:(p + 1) * page] = buf_ref[slot, p].astype(BF16)
    ns = m_ref.shape[0]
    hk = npg * page // ns
    for h in range(ns):
        feat = cb_ref[:, h * hk:(h + 1) * hk]
        accumulate(scores(feat), feat, h)

    @pl.when(c == pl.num_programs(1) - 1)
    def _():
        nw = new_ref[0].astype(BF16)
        accumulate(scores(nw) + nbias_ref[...], nw, 0)
        ctx = _merge_softmax_states(m_ref, l_ref, acc_ref).astype(BF16)
        full = jnp.dot(ctx, wuv_ref[...], preferred_element_type=F32)
        for h in range(MLA_HEADS):
            o_ref[0, h * per:(h + 1) * per, :] = full[h * per:(h + 1) * per, h * MLA_V:(h + 1) * MLA_V]


def _mla_decode(pool_t, layer, page_table, qbd, qr, wuk_t, wuv, new_rows_t, new_bias):
    nb, n_pages = page_table.shape
    w, page = pool_t.shape[2], pool_t.shape[3]
    r = qbd.shape[1]
    npg = min(MLA_DEC_PAGES, n_pages)
    assert n_pages % npg == 0 and page == LANES
    full = lambda a: pl.BlockSpec(a.shape, lambda b, c, pt: (0,) * a.ndim)
    return pl.pallas_call(
        functools.partial(_mla_decode_body, layer=layer, npg=npg, page=page),
        grid_spec=pltpu.PrefetchScalarGridSpec(
            num_scalar_prefetch=1,
            grid=(nb, n_pages // npg),
            in_specs=[pl.BlockSpec((1, r, qbd.shape[2]), lambda b, c, pt: (b, 0, 0)),
                      pl.BlockSpec((1, r, MLA_ROPE), lambda b, c, pt: (b, 0, 0)),
                      full(wuk_t), full(wuv),
                      pl.BlockSpec((1, w, NEW_PAD), lambda b, c, pt: (b, 0, 0)),
                      full(new_bias), pl.BlockSpec(memory_space=pl.ANY)],
            out_specs=pl.BlockSpec((1, r, MLA_V), lambda b, c, pt: (b, 0, 0)),
            scratch_shapes=[pltpu.VMEM((w, npg * page), BF16),
                            pltpu.VMEM((MLA_HEADS * MLA_NOPE + 2 * r, w), BF16),
                            pltpu.VMEM((DEC_SPLIT, r, LANES), F32), pltpu.VMEM((DEC_SPLIT, r, LANES), F32),
                            pltpu.VMEM((DEC_SPLIT, r, MLA_KV_RANK), F32)] + _page_scratch(npg, w, page),
        ),
        out_shape=jax.ShapeDtypeStruct((nb, r, MLA_V), F32),
        compiler_params=_cparams(("arbitrary", "arbitrary")),
        name="mla_decode",
    )(page_table, qbd, qr, wuk_t, wuv, new_rows_t, new_bias, pool_t)


def _merge_body(lam_ref, x_ref, ocmp_ref, osel_ref, owin_ref, gate_ref, omla_ref, odiff_ref, subln_ref,
                mg_ref, wb_ref, wo_ref, o_ref, *, post_scale):
    d = x_ref.shape[1]
    gate = gate_ref[...]
    nsa = []
    for h in range(NSA_HEADS):
        nsa.append(gate[:, 3 * h:3 * h + 1] * ocmp_ref[h] + gate[:, 3 * h + 1:3 * h + 2] * osel_ref[h]
                   + gate[:, 3 * h + 2:3 * h + 3] * owin_ref[h])
    br_nsa = jnp.concatenate(nsa, axis=-1)
    br_mla = jnp.concatenate([omla_ref[h] for h in range(MLA_HEADS)], axis=-1)
    lam = lam_ref[0]
    dif = []
    for h in range(DIFF_HEADS):
        a = odiff_ref[0, h] - lam * odiff_ref[1, h]
        dif.append(_rms(a, subln_ref[...]) * post_scale)
    br_diff = jnp.concatenate(dif, axis=-1)
    acc = None
    for b, br in enumerate((br_nsa, br_mla, br_diff)):
        term = mg_ref[:, b * d:(b + 1) * d] * jnp.dot(br.astype(BF16), wb_ref[b], preferred_element_type=F32)
        acc = term if acc is None else acc + term
    o_ref[...] = x_ref[...] + jnp.dot(acc.astype(BF16), wo_ref[...], preferred_element_type=F32)


def _merge(x, o_cmp, o_sel, o_win, gate, o_mla, o_diff, lam, subln, mg, wb, wo, post_scale, tm=256):
    t, d = x.shape
    tm = min(tm, t)
    heads = lambda a: pl.BlockSpec((a.shape[0], tm, a.shape[2]), lambda i: (0, i, 0))
    row = lambda w: pl.BlockSpec((tm, w), lambda i: (i, 0))
    full = lambda a: pl.BlockSpec(a.shape, lambda i: (0,) * a.ndim)
    return pl.pallas_call(
        functools.partial(_merge_body, post_scale=post_scale),
        grid=(t // tm,),
        in_specs=[pl.BlockSpec(memory_space=pltpu.SMEM), row(d), heads(o_cmp), heads(o_sel), heads(o_win),
                  row(LANES), heads(o_mla),
                  pl.BlockSpec((2, DIFF_HEADS, tm, 2 * DIFF_DH), lambda i: (0, 0, i, 0)),
                  full(subln), row(3 * d), full(wb), full(wo)],
        out_specs=row(d),
        out_shape=jax.ShapeDtypeStruct((t, d), F32),
        compiler_params=_cparams(("arbitrary",)),
        name="merge_branches",
    )(lam, x, o_cmp, o_sel, o_win, gate, o_mla, o_diff, subln, mg, wb, wo)


def _xattn_body(x_ref, g_ref, wq_ref, gq_ref, k_ref, v_ref, wo_ref, o_ref):
    x = x_ref[0]
    xn = _rms(x, g_ref[...]).astype(BF16)
    acc = x
    for h in range(MEM_HEADS):
        q = _rms(jnp.dot(xn, wq_ref[h], preferred_element_type=F32), gq_ref[...]).astype(BF16)
        s = lax.dot_general(q, k_ref[0, h], NT_DIMS, preferred_element_type=F32) * MEM_DH ** -0.5
        p = jnp.exp(s - jnp.max(s, axis=-1, keepdims=True))
        p = p / jnp.sum(p, axis=-1, keepdims=True)
        o = jnp.dot(p.astype(BF16), v_ref[0, h], preferred_element_type=F32).astype(BF16)
        acc = acc + jnp.dot(o, wo_ref[h], preferred_element_type=F32)
    o_ref[0] = acc


def _cross_attend(x3, g, wq, gq, k4, v4, wo):
    nbk = k4.shape[0]
    nb, tm, d = x3.shape
    kidx = (lambda i: (i, 0, 0, 0)) if nbk > 1 else (lambda i: (0, 0, 0, 0))
    full = lambda a: pl.BlockSpec(a.shape, lambda i: (0,) * a.ndim)
    return pl.pallas_call(
        _xattn_body,
        grid=(nb,),
        in_specs=[pl.BlockSpec((1, tm, d), lambda i: (i, 0, 0)), full(g), full(wq), full(gq),
                  pl.BlockSpec((1,) + k4.shape[1:], kidx), pl.BlockSpec((1,) + v4.shape[1:], kidx), full(wo)],
        out_specs=pl.BlockSpec((1, tm, d), lambda i: (i, 0, 0)),
        out_shape=jax.ShapeDtypeStruct(x3.shape, F32),
        compiler_params=_cparams(("arbitrary",)),
        name="memory_cross_attention",
    )(x3, g, wq, gq, k4, v4, wo)


def _ffn_body(x_ref, g_ref, wg_ref, wu_ref, wd_ref, o_ref, xn_ref, acc_ref):
    f = pl.program_id(1)

    @pl.when(f == 0)
    def _():
        xn_ref[...] = _rms(x_ref[...], g_ref[...]).astype(BF16)
        acc_ref[...] = jnp.zeros(acc_ref.shape, F32)

    xn = xn_ref[...]
    hid = (jax.nn.silu(jnp.dot(xn, wg_ref[...], preferred_element_type=F32))
           * jnp.dot(xn, wu_ref[...], preferred_element_type=F32))
    acc_ref[...] += jnp.dot(hid.astype(BF16), wd_ref[...], preferred_element_type=F32)

    @pl.when(f == pl.num_programs(1) - 1)
    def _():
        o_ref[...] = x_ref[...] + acc_ref[...]


def _ffn_tiles(t, ff):
    tm = min(512, t)
    tf = 896 if ff % 896 == 0 else ff
    return tm, tf


def _dense_ffn(x, g, wg, wu, wd):
    t, d = x.shape
    ff = wg.shape[1]
    tm, tf = _ffn_tiles(t, ff)
    return pl.pallas_call(
        _ffn_body,
        grid=(t // tm, ff // tf),
        in_specs=[pl.BlockSpec((tm, d), lambda i, f: (i, 0)), pl.BlockSpec((1, d), lambda i, f: (0, 0)),
                  pl.BlockSpec((d, tf), lambda i, f: (0, f)), pl.BlockSpec((d, tf), lambda i, f: (0, f)),
                  pl.BlockSpec((tf, d), lambda i, f: (f, 0))],
        out_specs=pl.BlockSpec((tm, d), lambda i, f: (i, 0)),
        out_shape=jax.ShapeDtypeStruct((t, d), F32),
        scratch_shapes=[pltpu.VMEM((tm, d), BF16), pltpu.VMEM((tm, d), F32)],
        compiler_params=_cparams(("arbitrary", "arbitrary")),
        name="dense_ffn",
    )(x, g, wg, wu, wd)


def _moe_body(x_ref, g_ref, wrh_ref, wrl_ref, wg_ref, wu_ref, wd_ref, o_ref, xn_ref, gate_ref, acc_ref):
    e = pl.program_id(1)
    f = pl.program_id(2)

    @pl.when((e == 0) & (f == 0))
    def _():
        xf = _rms(x_ref[...], g_ref[...])
        xh, xl = _split_hi_lo(xf)
        xn_ref[...] = xh
        acc_ref[...] = jnp.zeros(acc_ref.shape, F32)
        logits = (jnp.dot(xh, wrh_ref[...], preferred_element_type=F32)
                  + jnp.dot(xl, wrh_ref[...], preferred_element_type=F32)
                  + jnp.dot(xh, wrl_ref[...], preferred_element_type=F32))
        lane = lax.broadcasted_iota(I32, logits.shape, 1)
        logits = jnp.where(lane < N_EXPERTS, logits, -jnp.inf)
        m1 = jnp.max(logits, axis=-1, keepdims=True)
        i1 = jnp.min(jnp.where(logits == m1, lane, LANES), axis=-1, keepdims=True)
        rest = jnp.where(lane == i1, -jnp.inf, logits)
        m2 = jnp.max(rest, axis=-1, keepdims=True)
        i2 = jnp.min(jnp.where(rest == m2, lane, LANES), axis=-1, keepdims=True)
        e2 = jnp.exp(m2 - m1)
        w1 = 1.0 / (1.0 + e2)
        w2 = e2 / (1.0 + e2)
        gate_ref[...] = jnp.where(lane == i1, w1, 0.0) + jnp.where(lane == i2, w2, 0.0)

    xn = xn_ref[...]
    lane = lax.broadcasted_iota(I32, gate_ref.shape, 1)
    ge = jnp.sum(jnp.where(lane == e, gate_ref[...], 0.0), axis=-1, keepdims=True)
    hid = (jax.nn.silu(jnp.dot(xn, wg_ref[0], preferred_element_type=F32))
           * jnp.dot(xn, wu_ref[0], preferred_element_type=F32))
    acc_ref[...] += jnp.dot((hid * ge).astype(BF16), wd_ref[0], preferred_element_type=F32)

    @pl.when((e == pl.num_programs(1) - 1) & (f == pl.num_programs(2) - 1))
    def _():
        o_ref[...] = x_ref[...] + acc_ref[...]


def _moe_ffn(x, g, w_router, wg, wu, wd):
    t, d = x.shape
    ne, _, ff = wg.shape
    tm, tf = _ffn_tiles(t, ff)
    wr = jnp.pad(w_router.astype(F32), ((0, 0), (0, LANES - ne)))
    wrh = wr.astype(BF16)
    wrl = (wr - wrh.astype(F32)).astype(BF16)
    return pl.pallas_call(
        _moe_body,
        grid=(t // tm, ne, ff // tf),
        in_specs=[pl.BlockSpec((tm, d), lambda i, e, f: (i, 0)), pl.BlockSpec((1, d), lambda i, e, f: (0, 0)),
                  pl.BlockSpec((d, LANES), lambda i, e, f: (0, 0)), pl.BlockSpec((d, LANES), lambda i, e, f: (0, 0)),
                  pl.BlockSpec((1, d, tf), lambda i, e, f: (e, 0, f)),
                  pl.BlockSpec((1, d, tf), lambda i, e, f: (e, 0, f)),
                  pl.BlockSpec((1, tf, d), lambda i, e, f: (e, f, 0))],
        out_specs=pl.BlockSpec((tm, d), lambda i, e, f: (i, 0)),
        out_shape=jax.ShapeDtypeStruct((t, d), F32),
        scratch_shapes=[pltpu.VMEM((tm, d), BF16), pltpu.VMEM((tm, LANES), F32), pltpu.VMEM((tm, d), F32)],
        compiler_params=_cparams(("arbitrary", "arbitrary", "arbitrary")),
        name="moe_ffn",
    )(x, g, wrh, wrl, wg, wu, wd)


IN_SPLITS = (512, 384, 24, 256, 128, 32, 512, 128, 128, 3072)


def _layer_weights(l, p):
    d = p["w_in"].shape[1]
    w_in = p["w_in"][l]
    offs = np.cumsum((0,) + IN_SPLITS)
    seg = lambda i: w_in[:, offs[i]:offs[i + 1]]
    padl = lambda a, n: jnp.pad(a, ((0, 0), (0, n - a.shape[1])))
    w_a = jnp.concatenate([seg(0), seg(1), seg(3), seg(4), seg(6), seg(7), seg(8),
                           padl(seg(2), LANES), padl(seg(5), LANES)], axis=1).astype(BF16)
    half = MLA_ROPE // 2
    per = MLA_NOPE + MLA_ROPE
    wuq = p["w_mla_uq"][l].reshape(MLA_Q_RANK, MLA_HEADS, per)
    w_uq = jnp.concatenate([wuq[:, :, :MLA_NOPE].reshape(MLA_Q_RANK, -1),
                            wuq[:, :, MLA_NOPE:MLA_NOPE + half].reshape(MLA_Q_RANK, -1),
                            wuq[:, :, MLA_NOPE + half:].reshape(MLA_Q_RANK, -1)], axis=1).astype(BF16)
    gq, gk = p["mla_norm_q"][l], p["mla_norm_k"][l]
    ones64 = jnp.ones((NSA_DH,), F32)
    gains = jnp.concatenate([
        jnp.tile(p["nsa_q_norm"][l], NSA_HEADS),
        p["nsa_k_norm"][l, 1], ones64, p["nsa_k_norm"][l, 2], ones64,
        p["mla_q_rank_norm"][l], p["mla_kv_rank_norm"][l],
        jnp.tile(gq[:MLA_NOPE], MLA_HEADS),
        jnp.tile(gq[MLA_NOPE:MLA_NOPE + half], MLA_HEADS), jnp.tile(gq[MLA_NOPE + half:], MLA_HEADS),
        gk[MLA_NOPE:], jnp.zeros((LANES - MLA_ROPE,), F32),
        jnp.tile(p["diff_norm_q"][l].reshape(-1), DIFF_HEADS),
        p["diff_norm_k"][l].reshape(-1)]).astype(F32).reshape(1, GN_END)
    return {
        "g_mix": p["norm_mix"][l].reshape(1, d).astype(F32),
        "w_a": w_a, "w_uq": w_uq, "gains": gains,
        "w_mg": seg(9).astype(BF16),
        "gk_nope": gk[:MLA_NOPE].astype(F32),
        "w_uk": p["w_mla_uk"][l].astype(BF16), "w_uv": p["w_mla_uv"][l].astype(BF16),
        "g_cmp": p["nsa_k_norm"][l, 0],
        "subln": p["diff_subln"][l].reshape(1, -1).astype(F32),
        "wb": p["w_branch"][l].astype(BF16), "wo": p["w_out"][l].astype(BF16),
    }


def _heads_major(a, nh):
    t = a.shape[0]
    return a.reshape(t, nh, a.shape[1] // nh).transpose(1, 0, 2)


def _pad_cols(a, n):
    return jnp.pad(a, [(0, 0)] * (a.ndim - 1) + [(0, n - a.shape[-1])])


def _mla_q_heads(qn, qr):
    half = MLA_ROPE // 2
    t = qn.shape[0]
    parts = [qn.reshape(t, MLA_HEADS, MLA_NOPE), qr[:, :LANES].reshape(t, MLA_HEADS, half),
             qr[:, LANES:].reshape(t, MLA_HEADS, half)]
    return _pad_cols(jnp.concatenate(parts, axis=-1), LANES).transpose(1, 0, 2)


def kernel(x_prompt, x_sample, cache_nsa_cmp, cache_nsa_sel, cache_nsa_win, cache_mla, cache_diff, cache_mem, page_table, mem_prompt, rel_bias_table, norm_mix, w_in, nsa_q_norm, nsa_k_norm, mla_q_rank_norm, mla_kv_rank_norm, w_mla_uq, w_mla_uk, w_mla_uv, mla_norm_q, mla_norm_k, diff_norm_q, diff_norm_k, diff_lambda, diff_subln, w_branch, w_out, norm_mem_x, norm_mem_m, w_mem_q, w_mem_kv, mem_norm_qk, w_mem_o, norm_ffn, w_ffn_gate, w_ffn_up, w_ffn_down, w_router, w_exp_gate, w_exp_up, w_exp_down):
    prm = dict(w_in=w_in, norm_mix=norm_mix, nsa_q_norm=nsa_q_norm, nsa_k_norm=nsa_k_norm,
               mla_q_rank_norm=mla_q_rank_norm, mla_kv_rank_norm=mla_kv_rank_norm, w_mla_uq=w_mla_uq,
               w_mla_uk=w_mla_uk, w_mla_uv=w_mla_uv, mla_norm_q=mla_norm_q, mla_norm_k=mla_norm_k,
               diff_norm_q=diff_norm_q, diff_norm_k=diff_norm_k, diff_subln=diff_subln,
               w_branch=w_branch, w_out=w_out)
    bp, sp, d = x_prompt.shape
    db, ds, _ = x_sample.shape
    depth = w_in.shape[0]
    n_pages = page_table.shape[1]
    page = cache_nsa_cmp.shape[2]
    past = n_pages * page
    wb_len = cache_nsa_win.shape[2]
    n_mem = mem_prompt.shape[1]
    assert bp == 1 and ds <= DEC_Q and past % CMP_BLOCK == 0 and wb_len % page == 0
    half = MLA_ROPE // 2

    tbl = rel_bias_table.astype(F32)
    tbl0 = jnp.zeros_like(tbl)
    tq_a, tq_w, tq_m = min(ATT_TQ, sp), min(WIN_TQ, sp), min(MLA_TQ, sp)
    dclamp = _round_up(BIAS_FLAT_DIST + LANES - 1, SUBLANES)
    dclamp_w = _round_up(WINDOW - 1 + LANES, SUBLANES)

    def prompt_strip(table, hoff, nh, tq, dcl, dmax=2 ** 30):
        return _bias_strip(table, hoff, nh, _round_up(dcl + 2 * tq, 256), LANES, 1, -1, -tq, dmax=dmax, mult=LOG2E)

    s_sel = prompt_strip(tbl, 0, NSA_HEADS, tq_a, dclamp)
    s_diff = prompt_strip(tbl, NSA_HEADS, DIFF_HEADS, tq_a, dclamp)
    s_win = prompt_strip(tbl, 0, NSA_HEADS, tq_w, dclamp_w, dmax=WINDOW - 1)
    s_mla = prompt_strip(tbl0, 0, 1, tq_m, LANES)
    rows_hq = lambda a: a.reshape(a.shape[0] * a.shape[1], a.shape[2])
    b_sel = rows_hq(_bias_strip(tbl, 0, NSA_HEADS, DEC_Q, past, 1, -1, past))
    b_diff1 = _bias_strip(tbl, NSA_HEADS, DIFF_HEADS, DEC_Q, past, 1, -1, past)
    b_diff = rows_hq(jnp.concatenate([b_diff1, b_diff1], axis=0))
    b_win = rows_hq(_bias_strip(tbl, 0, NSA_HEADS, DEC_Q, wb_len, 1, -1, wb_len, dmax=WINDOW - 1))
    nb_nsa = rows_hq(_bias_strip(tbl, 0, NSA_HEADS, DEC_Q, NEW_PAD, 1, -1, 0, cmax=ds))
    nb_diff1 = _bias_strip(tbl, NSA_HEADS, DIFF_HEADS, DEC_Q, NEW_PAD, 1, -1, 0, cmax=ds)
    nb_diff = rows_hq(jnp.concatenate([nb_diff1, nb_diff1], axis=0))
    nb_mla1 = _bias_strip(tbl0, 0, 1, DEC_Q, NEW_PAD, 1, -1, 0, cmax=ds)
    nb_mla = rows_hq(jnp.tile(nb_mla1, (MLA_HEADS, 1, 1)))

    nblk_p = sp // CMP_BLOCK
    ncol_p = max(SEL_COLS_MIN, _round_up(nblk_p, LANES))
    tkb_a = min(ATT_TKB, sp)
    nbk = tkb_a // CMP_BLOCK
    assert NSA_DH + nbk <= LANES
    own = (jnp.arange(sp, dtype=I32)[:, None] % tkb_a) // CMP_BLOCK == jnp.arange(nbk, dtype=I32)[None]
    sel_kfeat = jnp.where(own, NEG, 0.0).astype(BF16)
    nblk_s = past // CMP_BLOCK
    ncol_s = max(SEL_COLS_MIN, _round_up(nblk_s, LANES))
    exp_s = (jnp.arange(ncol_s, dtype=I32)[:, None] == (jnp.arange(past, dtype=I32) // CMP_BLOCK)[None]).astype(BF16)

    xp = x_prompt.reshape(sp, d)
    xs = jnp.pad(x_sample, ((0, 0), (0, DEC_Q - ds), (0, 0))).reshape(db * DEC_Q, d)
    pos_p = jnp.arange(sp, dtype=I32)
    pos_s = jnp.tile(past + jnp.arange(DEC_Q, dtype=I32), db)
    pt = page_table.astype(I32)
    pt_prompt = jnp.arange(sp // page, dtype=I32)[None]
    pt_win = (jnp.arange(db, dtype=I32)[:, None] * (wb_len // page) + jnp.arange(wb_len // page, dtype=I32)[None])
    win_pool = cache_nsa_win.reshape(depth, db * (wb_len // page), page, cache_nsa_win.shape[3])
    mla_pool_t = jnp.swapaxes(cache_mla, 2, 3)

    outs = {k: [] for k in ("p_cmp", "p_sel", "p_win", "p_mla", "p_diff", "p_mem",
                            "s_cmp", "s_sel", "s_win", "s_mla", "s_diff")}
    for l in range(depth):
        lw = _layer_weights(l, prm)
        lam_init = 0.8 - 0.6 * math.exp(-0.3 * l)
        lmb = diff_lambda[l].astype(F32)
        lam = (jnp.exp(jnp.sum(lmb[0] * lmb[1])) - jnp.exp(jnp.sum(lmb[2] * lmb[3])) + lam_init).reshape(1)

        fp = _features(xp, pos_p, lw)
        fs = _features(xs, pos_s, lw)
        mg_p = _mm(xp, lw["w_mg"], g=lw["g_mix"], epi="sigmoid", tn=768)
        mg_s = _mm(xs, lw["w_mg"], g=lw["g_mix"], epi="sigmoid", tn=768)

        nq_h = _heads_major(fp["nq"], NSA_HEADS)[None]
        q_nsa = (nq_h * NSA_DH ** -0.5).astype(BF16)
        q_nsa2 = (nq_h * (NSA_DH ** -0.5 * LOG2E)).astype(BF16)
        bm = _block_means(fp["cmp"].reshape(1, sp // page, page, 2 * NSA_DH), 0, pt_prompt, lw["g_cmp"])
        bm = jnp.pad(bm, ((0, 0), (0, ncol_p - nblk_p), (0, 0))).astype(BF16)
        o_cmp, chosen = _cmp_attention(tbl, q_nsa, bm[:, :, :NSA_DH], bm[:, :, NSA_DH:], tq=min(CMP_TQ, sp),
                                       qbase=0, n_cmp=nblk_p, add_cur=True)
        sel16 = fp["sel"].astype(BF16)
        not_sel = (1.0 - chosen[0, :, :nblk_p]).reshape(sp, sp // tkb_a, nbk).transpose(1, 0, 2)
        q_add = jnp.pad(not_sel, ((0, 0), (0, 0), (NSA_DH, LANES - NSA_DH - nbk))).astype(BF16)
        k_sel = _pad_cols(jnp.concatenate([sel16[:, :NSA_DH], sel_kfeat], axis=-1), LANES)[None]
        o_sel = _flash(_pad_cols(q_nsa2, LANES), k_sel, sel16[None, :, NSA_DH:], s_sel, tq=ATT_TQ, tkb=ATT_TKB,
                       scale=1.0, dclamp=dclamp, qadd=q_add)
        win16 = fp["win"].astype(BF16)
        o_win = _flash(q_nsa2, win16[None, :, :NSA_DH], win16[None, :, NSA_DH:], s_win, tq=WIN_TQ, tkb=WIN_TKB,
                       scale=1.0, dclamp=dclamp_w, window=WINDOW)
        lat_p = fp["mla"][:, :MLA_KV_RANK]
        kn = _mm(lat_p, lw["w_uk"], epi="grms", gsize=MLA_NOPE, gain=jnp.tile(lw["gk_nope"], MLA_HEADS),
                 out_dtype=BF16)
        vm = _mm(lat_p, lw["w_uv"], out_dtype=BF16)
        kr16 = fp["mla"][:, MLA_KV_RANK:].astype(BF16)
        k_mla = jnp.concatenate([_heads_major(kn, MLA_HEADS),
                                 jnp.broadcast_to(kr16[None], (MLA_HEADS, sp, MLA_ROPE))], axis=-1)
        k_mla = _pad_cols(k_mla, LANES)
        q_mla = _mla_q_heads(fp["qn"], fp["qr"]).astype(BF16)[:, None]
        o_mla = _flash(q_mla, k_mla, _heads_major(vm, MLA_HEADS), s_mla, tq=MLA_TQ, tkb=MLA_TKB,
                       scale=(MLA_NOPE + MLA_ROPE) ** -0.5 * LOG2E, dclamp=LANES)
        dq = (fp["dq"].reshape(sp, DIFF_HEADS, 2, DIFF_DH).transpose(2, 1, 0, 3)
              * (DIFF_DH ** -0.5 * LOG2E)).astype(BF16)
        dk = fp["diff"][:, :2 * DIFF_DH].reshape(sp, 2, DIFF_DH).transpose(1, 0, 2).astype(BF16)
        dv = fp["diff"][None, :, 2 * DIFF_DH:].astype(BF16)
        o_diff = _flash(dq, dk, dv, s_diff, tq=ATT_TQ, tkb=ATT_TKB, scale=1.0, dclamp=dclamp)
        xp = _merge(xp, o_cmp[0], o_sel[0], o_win[0], fp["gate"], o_mla[:, 0], o_diff, lam, lw["subln"], mg_p,
                    lw["wb"], lw["wo"], 1.0 - lam_init)

        def new_rows(a):
            return jnp.pad(a.reshape(db, DEC_Q, a.shape[1])[:, :ds], ((0, 0), (0, NEW_PAD - ds), (0, 0)))

        nq_s = fs["nq"].reshape(db, DEC_Q, NSA_HEADS, NSA_DH).transpose(0, 2, 1, 3) * NSA_DH ** -0.5
        q_s4 = nq_s.astype(BF16)
        q_s = _pad_cols(nq_s.reshape(db, NSA_HEADS * DEC_Q, NSA_DH), LANES).astype(BF16)
        bm_s = _block_means(cache_nsa_cmp, l, pt, lw["g_cmp"])
        bm_s = jnp.pad(bm_s, ((0, 0), (0, ncol_s - nblk_s), (0, 0))).astype(BF16)
        o_cmp_s, chosen_s = _cmp_attention(tbl, q_s4, bm_s[:, :, :NSA_DH], bm_s[:, :, NSA_DH:], tq=DEC_Q,
                                           qbase=past, n_cmp=nblk_s, add_cur=False,
                                           sb=CMP_DEC_BATCH if db % CMP_DEC_BATCH == 0 else 1)
        o_sel_s = _decode_attention(cache_nsa_sel, l, pt, q_s, b_sel, new_rows(fs["sel"]), nb_nsa,
                                    kw=LANES, vlo=0, vhi=LANES, nh=NSA_HEADS, sel=chosen_s, expand=exp_s)
        o_win_s = _decode_attention(win_pool, l, pt_win, q_s, b_win, new_rows(fs["win"]), nb_nsa,
                                    kw=LANES, vlo=0, vhi=LANES, nh=NSA_HEADS)
        tok_major = lambda a: a.reshape(db, NSA_HEADS, DEC_Q, NSA_DH).transpose(1, 0, 2, 3).reshape(
            NSA_HEADS, db * DEC_Q, NSA_DH)
        o_sel_s = tok_major(o_sel_s[:, :, NSA_DH:])
        o_win_s = tok_major(o_win_s[:, :, NSA_DH:])
        o_cmp_s = o_cmp_s.transpose(1, 0, 2, 3).reshape(NSA_HEADS, db * DEC_Q, NSA_DH)
        qn_s = (fs["qn"] * jnp.tile(lw["gk_nope"], MLA_HEADS)).reshape(db, DEC_Q, MLA_HEADS, MLA_NOPE)
        qn_s = qn_s.transpose(0, 2, 1, 3)
        qbd = (qn_s[:, :, :, None, :] * jnp.eye(MLA_HEADS, dtype=F32)[None, :, None, :, None]).reshape(
            db, MLA_HEADS * DEC_Q, MLA_HEADS * MLA_NOPE).astype(BF16)
        qr_s = jnp.concatenate([fs["qr"][:, :LANES].reshape(db, DEC_Q, MLA_HEADS, half),
                                fs["qr"][:, LANES:].reshape(db, DEC_Q, MLA_HEADS, half)], axis=-1)
        qr_s = qr_s.transpose(0, 2, 1, 3).reshape(db, MLA_HEADS * DEC_Q, MLA_ROPE).astype(BF16)
        o_mla_s = _mla_decode(mla_pool_t, l, pt, qbd, qr_s, lw["w_uk"].T, lw["w_uv"],
                              new_rows(fs["mla"]).transpose(0, 2, 1), nb_mla)
        o_mla_s = o_mla_s.reshape(db, MLA_HEADS, DEC_Q, MLA_V).transpose(1, 0, 2, 3).reshape(
            MLA_HEADS, db * DEC_Q, MLA_V)
        dq_s = fs["dq"].reshape(db, DEC_Q, DIFF_HEADS, 2, DIFF_DH).transpose(0, 3, 2, 1, 4) * DIFF_DH ** -0.5
        dq_s = dq_s.reshape(db, 2, DIFF_HEADS * DEC_Q, DIFF_DH)
        zero = jnp.zeros_like(dq_s[:, 0])
        q_d = jnp.concatenate([jnp.concatenate([dq_s[:, 0], zero], axis=-1),
                               jnp.concatenate([zero, dq_s[:, 1]], axis=-1)], axis=1).astype(BF16)
        o_diff_s = _decode_attention(cache_diff, l, pt, q_d, b_diff, new_rows(fs["diff"]), nb_diff,
                                     kw=2 * DIFF_DH, vlo=2 * DIFF_DH, vhi=4 * DIFF_DH, nh=2 * DIFF_HEADS)
        o_diff_s = o_diff_s.reshape(db, 2, DIFF_HEADS, DEC_Q, 2 * DIFF_DH).transpose(1, 2, 0, 3, 4).reshape(
            2, DIFF_HEADS, db * DEC_Q, 2 * DIFF_DH)
        xs = _merge(xs, o_cmp_s, o_sel_s, o_win_s, fs["gate"], o_mla_s, o_diff_s, lam, lw["subln"], mg_s,
                    lw["wb"], lw["wo"], 1.0 - lam_init)

        g_mem_k = jnp.tile(mem_norm_qk[l, 1], MEM_HEADS)
        hw = MEM_HEADS * MEM_DH
        k_mem = _mm(mem_prompt[0], w_mem_kv[l][:, :hw], g=norm_mem_m[l], epi="grms", gsize=MEM_DH, gain=g_mem_k)
        v_mem = _mm(mem_prompt[0], w_mem_kv[l][:, hw:], g=norm_mem_m[l])
        kv_p = jnp.concatenate([k_mem, v_mem], axis=-1)
        mem_heads = lambda a: a.reshape(a.shape[0], n_mem, MEM_HEADS, MEM_DH).transpose(0, 2, 1, 3).astype(BF16)
        wq4 = w_mem_q[l].reshape(d, MEM_HEADS, MEM_DH).transpose(1, 0, 2).astype(BF16)
        wo4 = w_mem_o[l].reshape(MEM_HEADS, MEM_DH, d).astype(BF16)
        g_x = norm_mem_x[l].reshape(1, d).astype(F32)
        g_q = mem_norm_qk[l, 0].reshape(1, MEM_DH).astype(F32)
        tmx = min(256, sp)
        xp = _cross_attend(xp.reshape(sp // tmx, tmx, d), g_x, wq4, g_q, mem_heads(k_mem[None]),
                           mem_heads(v_mem[None]), wo4).reshape(sp, d)
        cm = cache_mem[l]
        xs = _cross_attend(xs.reshape(db, DEC_Q, d), g_x, wq4, g_q, mem_heads(cm[:, :, :hw]),
                           mem_heads(cm[:, :, hw:]), wo4).reshape(db * DEC_Q, d)

        g_f = norm_ffn[l].reshape(1, d).astype(F32)
        j = l // 2
        if l % 2 == 0:
            wg, wu, wd = w_ffn_gate[j].astype(BF16), w_ffn_up[j].astype(BF16), w_ffn_down[j].astype(BF16)
            xp = _dense_ffn(xp, g_f, wg, wu, wd)
            xs = _dense_ffn(xs, g_f, wg, wu, wd)
        else:
            wg, wu, wd = w_exp_gate[j].astype(BF16), w_exp_up[j].astype(BF16), w_exp_down[j].astype(BF16)
            xp = _moe_ffn(xp, g_f, w_router[j], wg, wu, wd)
            xs = _moe_ffn(xs, g_f, w_router[j], wg, wu, wd)

        outs["p_cmp"].append(fp["cmp"][None])
        outs["p_sel"].append(fp["sel"][None])
        outs["p_win"].append(fp["win"][None, -min(WINDOW, sp):])
        outs["p_mla"].append(fp["mla"][None])
        outs["p_diff"].append(fp["diff"][None])
        outs["p_mem"].append(kv_p[None])
        srows = lambda a: a.reshape(db, DEC_Q, a.shape[1])[:, :ds]
        outs["s_cmp"].append(srows(fs["cmp"]))
        outs["s_sel"].append(srows(fs["sel"]))
        outs["s_win"].append(srows(fs["win"]))
        outs["s_mla"].append(srows(fs["mla"]))
        outs["s_diff"].append(srows(fs["diff"]))

    y_p = xp.reshape(bp, sp, d)
    y_s = xs.reshape(db, DEC_Q, d)[:, :ds]
    st = lambda k: jnp.stack(outs[k])
    return (y_p, y_s, st("p_cmp"), st("p_sel"), st("p_win"), st("p_mla"), st("p_diff"), st("p_mem"),
            st("s_cmp"), st("s_sel"), st("s_win"), st("s_mla"), st("s_diff"))
```
